```python
import math
import jax, jax.numpy as jnp
from jax import lax
import numpy as np


D_MODEL = 1024
BATCH = 1
SEQ = 16384
DEPTH = 2
DEC_BATCH = 16
DEC_SEQ = 4096
PAST_LEN = 128

A_HEADS = 8
A_DK = 64
A_DV = 64
A_CONV = 3
A_CHUNK = 64
B_HEADS = 4
B_DK = 32
B_DV = 64
B_RANK = 16
B_TAU = 16.0
B_CHUNK = 16
C_HEADS = 4
C_DK = 64
C_DV = 64
C_CHUNK = 64
D_FF = 2688
FFN_CONV = 3
EPS = 1e-6

A_QK = A_HEADS * A_DK
A_VW = A_HEADS * A_DV
A_CONV_CH = 2 * A_QK + A_VW
B_QK = B_HEADS * B_DK
B_VW = B_HEADS * B_DV
C_QK = C_HEADS * C_DK
C_VW = C_HEADS * C_DV
MIX_WIDTH = A_VW + B_VW + C_VW
IN_SIZES = (A_CONV_CH, A_VW, 2 * A_HEADS, 2 * A_HEADS,
            B_QK, B_QK, B_VW, B_VW, 2 * B_RANK,
            C_QK, C_QK, C_VW, C_VW, 2 * C_HEADS, 2 * C_HEADS)
IN_WIDTH = sum(IN_SIZES)

kernel_name = 'hybrid_bidir_encoder'


def rmsnorm(x, g):
    xf = x.astype(jnp.float32)
    y = xf * lax.rsqrt(jnp.mean(xf * xf, axis=-1, keepdims=True) + EPS)
    return (y * g.astype(jnp.float32)).astype(x.dtype)


def l2norm(x):
    return x * lax.rsqrt(jnp.sum(x * x, axis=-1, keepdims=True) + EPS)


def dwconv_centred(x, w):
    K, Ch = w.shape
    return lax.conv_general_dilated(x, w[:, None, :], window_strides=(1,), padding=[(K // 2, K // 2)],
                                    dimension_numbers=('NWC', 'WIO', 'NWC'), feature_group_count=Ch)


def chunkify(x, c):
    b, s = x.shape[:2]
    x = x.reshape((b, s // c, c) + x.shape[2:])
    return jnp.moveaxis(x, 3, 1)


def unchunk(o):
    b, h, n, c, d = o.shape
    return jnp.moveaxis(o, 1, 3).reshape(b, n * c, h, d)


def run_direction(fn, chunk, reverse, *arrs):
    if reverse:
        arrs = [jnp.flip(a, axis=1) for a in arrs]
    o = unchunk(fn(*[chunkify(a, chunk) for a in arrs]))
    return jnp.flip(o, axis=1) if reverse else o


def scan_chunks(step, init, xs):
    xs = tuple(jnp.moveaxis(a, 2, 0) for a in xs)
    _, o = lax.scan(step, init, xs)
    return jnp.moveaxis(o, 0, 2)


def gated_delta_direction(q, k, v, g, beta):
    cn = q.shape[3]
    causal = jnp.tril(jnp.ones((cn, cn), bool))
    strict = jnp.tril(jnp.ones((cn, cn), bool), k=-1)
    gam = jnp.cumsum(g, axis=-1)
    decay = jnp.exp(jnp.where(causal, gam[..., :, None] - gam[..., None, :], -jnp.inf))
    kb = k * beta[..., None]
    m = jnp.where(strict, jnp.einsum('bhnid,bhnjd->bhnij', kb, k) * decay, 0.0)
    rhs = jnp.concatenate([v * beta[..., None], kb * jnp.exp(gam)[..., None]], axis=-1)
    sol = lax.linalg.triangular_solve(m + jnp.eye(cn, dtype=m.dtype), rhs, left_side=True, lower=True,
                                      unit_diagonal=True)
    w_val, k_cum = sol[..., :v.shape[-1]], sol[..., v.shape[-1]:]
    aqk = jnp.einsum('bhnid,bhnjd->bhnij', q, k) * decay
    q_dec = q * jnp.exp(gam)[..., None]
    k_dec = k * jnp.exp(gam[..., -1:] - gam)[..., None]
    c_dec = jnp.exp(gam[..., -1])

    def step(s, xs):
        wv, kc, at, qd, kd, cd = xs
        v_new = wv - jnp.einsum('bhck,bhkv->bhcv', kc, s)
        o = jnp.einsum('bhck,bhkv->bhcv', qd, s) + jnp.einsum('bhij,bhjv->bhiv', at, v_new)
        s = s * cd[..., None, None] + jnp.einsum('bhck,bhcv->bhkv', kd, v_new)
        return s, o

    s0 = jnp.zeros(q.shape[:2] + (q.shape[-1], v.shape[-1]), q.dtype)
    return scan_chunks(step, s0, (w_val, k_cum, aqk, q_dec, k_dec, c_dec))


def gla_direction(q, k, v, g):
    cn = q.shape[3]
    causal = jnp.tril(jnp.ones((cn, cn), bool))
    b = jnp.cumsum(g, axis=-2)
    rel = jnp.where(causal[:, :, None], b[..., :, None, :] - b[..., None, :, :], -jnp.inf)
    attn = jnp.sum(q[..., :, None, :] * k[..., None, :, :] * jnp.exp(rel), axis=-1)
    q_dec = q * jnp.exp(b)
    k_dec = k * jnp.exp(b[..., -1:, :] - b)
    c_dec = jnp.exp(b[..., -1, :])

    def step(s, xs):
        qd, kd, vv, at, cd = xs
        o = jnp.einsum('bhck,bhkv->bhcv', qd, s) + jnp.einsum('bhij,bhjv->bhiv', at, vv)
        s = cd[..., None] * s + jnp.einsum('bhck,bhcv->bhkv', kd, vv)
        return s, o

    s0 = jnp.zeros(q.shape[:2] + (q.shape[-1], v.shape[-1]), q.dtype)
    return scan_chunks(step, s0, (q_dec, k_dec, v, attn, c_dec))


def mlstm_direction(q, k, v, i_pre, lf):
    cn = q.shape[3]
    causal = jnp.tril(jnp.ones((cn, cn), bool))
    fc = jnp.cumsum(lf, axis=-1)
    dlog = jnp.where(causal, fc[..., :, None] - fc[..., None, :] + i_pre[..., None, :], -jnp.inf)
    dmax = jnp.max(dlog, axis=-1)
    p = jnp.einsum('bhnid,bhnjd->bhnij', q, k) * jnp.exp(dlog - dmax[..., None])
    glog = fc[..., -1:] - fc + i_pre
    gmax = jnp.max(glog, axis=-1)
    f_last = fc[..., -1]

    def step(carry, xs):
        cs, ns, m = carry
        qc, kc, vc, pc, dm, fcc, gc, gm, fl = xs
        m_inter = fcc + m[..., None]
        m_i = jnp.maximum(dm, m_inter)
        s_inter = jnp.exp(m_inter - m_i)
        s_intra = jnp.exp(dm - m_i)
        num = (s_inter[..., None] * jnp.einsum('bhck,bhkv->bhcv', qc, cs)
               + s_intra[..., None] * jnp.einsum('bhij,bhjv->bhiv', pc, vc))
        den = s_inter * jnp.einsum('bhck,bhk->bhc', qc, ns) + s_intra * jnp.sum(pc, axis=-1)
        h = num / jnp.maximum(jnp.abs(den), jnp.exp(-m_i))[..., None]
        m_new = jnp.maximum(fl + m, gm)
        a = jnp.exp(fl + m - m_new)
        kw = kc * jnp.exp(gc - m_new[..., None])[..., None]
        cs = a[..., None, None] * cs + jnp.einsum('bhck,bhcv->bhkv', kw, vc)
        ns = a[..., None] * ns + jnp.sum(kw, axis=-2)
        return (cs, ns, m_new), h

    bh = q.shape[:2]
    init = (jnp.zeros(bh + (q.shape[-1], v.shape[-1]), q.dtype), jnp.zeros(bh + (q.shape[-1],), q.dtype),
            jnp.zeros(bh, q.dtype))
    return scan_chunks(step, init, (q, k, v, p, dmax, fc, glog, gmax, f_last))


def hybrid_mixer(h, w_in, conv_a, a_log, dt_bias, norm_a, w_gla_up, b_gla, norm_b, bias_ig, bias_fg, norm_c, w_out):
    bn, s, _ = h.shape
    f32 = jnp.float32
    proj = h @ w_in
    split_at = [int(v) for v in np.cumsum(IN_SIZES)[:-1]]
    (a_qkv, a_z, a_alpha, a_beta, b_q, b_k, b_v, b_g, b_lr,
     c_q, c_k, c_v, c_o, c_i, c_f) = jnp.split(proj, split_at, axis=-1)

    a_qkv = jax.nn.silu(dwconv_centred(a_qkv, conv_a.astype(a_qkv.dtype))).astype(f32)
    aq, ak, av = jnp.split(a_qkv, [A_QK, 2 * A_QK], axis=-1)
    aq = l2norm(aq.reshape(bn, s, A_HEADS, A_DK)) * (A_DK ** -0.5)
    ak = l2norm(ak.reshape(bn, s, A_HEADS, A_DK))
    av = av.reshape(bn, s, A_HEADS, A_DV)
    a_alpha = a_alpha.astype(f32).reshape(bn, s, 2, A_HEADS)
    a_beta = a_beta.astype(f32).reshape(bn, s, 2, A_HEADS)
    o_a = 0.0
    for d in range(2):
        g = -jnp.exp(a_log[d].astype(f32)) * jax.nn.softplus(a_alpha[:, :, d] + dt_bias[d].astype(f32))
        beta = jax.nn.sigmoid(a_beta[:, :, d])
        o_a = o_a + run_direction(gated_delta_direction, A_CHUNK, d == 1, aq, ak, av, g, beta)
    o_a = rmsnorm(o_a, norm_a) * jax.nn.silu(a_z.astype(f32).reshape(bn, s, A_HEADS, A_DV))

    bq = b_q.astype(f32).reshape(bn, s, B_HEADS, B_DK) * (B_DK ** -0.5)
    bk = b_k.astype(f32).reshape(bn, s, B_HEADS, B_DK)
    bv = b_v.astype(f32).reshape(bn, s, B_HEADS, B_DV)
    b_lr = b_lr.astype(f32).reshape(bn, s, 2, B_RANK)
    o_b = 0.0
    for d in range(2):
        gk = jax.nn.log_sigmoid(b_lr[:, :, d] @ w_gla_up[d].astype(f32) + b_gla[d].astype(f32)) / B_TAU
        gk = gk.reshape(bn, s, B_HEADS, B_DK)
        o_b = o_b + run_direction(gla_direction, B_CHUNK, d == 1, bq, bk, bv, gk)
    o_b = rmsnorm(o_b, norm_b) * jax.nn.silu(b_g.astype(f32).reshape(bn, s, B_HEADS, B_DV))

    cq = c_q.astype(f32).reshape(bn, s, C_HEADS, C_DK)
    ck = c_k.astype(f32).reshape(bn, s, C_HEADS, C_DK) * (C_DK ** -0.5)
    cv = c_v.astype(f32).reshape(bn, s, C_HEADS, C_DV)
    c_i = c_i.astype(f32).reshape(bn, s, 2, C_HEADS)
    c_f = c_f.astype(f32).reshape(bn, s, 2, C_HEADS)
    h_c = 0.0
    for d in range(2):
        ip = c_i[:, :, d] + bias_ig[d].astype(f32)
        lf = jax.nn.log_sigmoid(c_f[:, :, d] + bias_fg[d].astype(f32))
        h_c = h_c + run_direction(mlstm_direction, C_CHUNK, d == 1, cq, ck, cv, ip, lf)
    h_c = jax.nn.sigmoid(c_o.astype(f32).reshape(bn, s, C_HEADS, C_DV)) * h_c
    mu = jnp.mean(h_c, axis=-1, keepdims=True)
    var = jnp.mean(jnp.square(h_c - mu), axis=-1, keepdims=True)
    o_c = (h_c - mu) * lax.rsqrt(var + EPS) * norm_c.astype(f32).reshape(C_HEADS, C_DV)

    mixed = jnp.concatenate([o_a.reshape(bn, s, A_VW), o_b.reshape(bn, s, B_VW), o_c.reshape(bn, s, C_VW)],
                            axis=-1).astype(h.dtype)
    return mixed @ w_out


def conv_ffn(h, w_up, conv_w, conv_b, w_down):
    u = dwconv_centred(h @ w_up, conv_w.astype(h.dtype)) + conv_b
    gate, val = jnp.split(u, 2, axis=-1)
    return (jax.nn.gelu(gate, approximate=False) * val) @ w_down


def encoder_layer(x, c, w_ada, b_ada, g_pre_mix, w_in, conv_a, a_log, dt_bias, norm_a, w_gla_up, b_gla,
                  norm_b, bias_ig, bias_fg, norm_c, w_out, g_post_mix, g_pre_ffn, w_up, conv_w, conv_b,
                  w_down, g_post_ffn):
    mod = (jax.nn.silu(c) @ w_ada + b_ada)[:, None, :]
    sh1, sc1, gt1, sh2, sc2, gt2 = jnp.split(mod, 6, axis=-1)
    h = rmsnorm(x, g_pre_mix) * (1.0 + sc1) + sh1
    x = x + gt1 * rmsnorm(hybrid_mixer(h, w_in, conv_a, a_log, dt_bias, norm_a, w_gla_up, b_gla, norm_b,
                                       bias_ig, bias_fg, norm_c, w_out), g_post_mix)
    h = rmsnorm(x, g_pre_ffn) * (1.0 + sc2) + sh2
    x = x + gt2 * rmsnorm(conv_ffn(h, w_up, conv_w, conv_b, w_down), g_post_ffn)
    return x


def trunk(x, c, weights):
    for l in range(DEPTH):
        x = encoder_layer(x, c, *[w[l] for w in weights])
    return x


def setup_inputs(seed: int = 0) -> dict:
    key = jax.random.key(seed)
    ks = jax.random.split(key, 32)
    L, D = DEPTH, D_MODEL

    def nrm(k, shape, scale):
        return jax.random.normal(k, shape, jnp.float32) * scale

    dt = jnp.exp(jax.random.uniform(ks[8], (L, 2, A_HEADS), jnp.float32, math.log(1e-3), math.log(1e-1)))
    return {
        'x_prompt': nrm(ks[0], (BATCH, SEQ, D), 1.0),
        'x_sample': nrm(ks[1], (DEC_BATCH, DEC_SEQ, D), 1.0),
        'c_prompt': nrm(ks[2], (BATCH, D), 1.0),
        'c_sample': nrm(ks[3], (DEC_BATCH, D), 1.0),
        'w_ada': nrm(ks[4], (L, D, 6 * D), 0.5 * D ** -0.5),
        'b_ada': nrm(ks[5], (L, 6 * D), 0.02),
        'g_pre_mix': 1.0 + nrm(ks[6], (L, D), 0.02),
        'w_in': nrm(ks[9], (L, D, IN_WIDTH), D ** -0.5),
        'conv_a': nrm(ks[10], (L, A_CONV, A_CONV_CH), A_CONV ** -0.5),
        'a_log': jnp.log(jax.random.uniform(ks[7], (L, 2, A_HEADS), jnp.float32, 1.0, 16.0)),
        'dt_bias': dt + jnp.log(-jnp.expm1(-dt)),
        'norm_a': 1.0 + nrm(ks[11], (L, A_DV), 0.02),
        'w_gla_up': nrm(ks[12], (L, 2, B_RANK, B_QK), B_RANK ** -0.5),
        'b_gla': nrm(ks[13], (L, 2, B_QK), 0.1),
        'norm_b': 1.0 + nrm(ks[14], (L, B_DV), 0.02),
        'bias_ig': nrm(ks[15], (L, 2, C_HEADS), 0.1),
        'bias_fg': 3.0 + 3.0 * jax.random.uniform(ks[16], (L, 2, C_HEADS), jnp.float32),
        'norm_c': 1.0 + nrm(ks[17], (L, C_VW), 0.02),
        'w_out': nrm(ks[18], (L, MIX_WIDTH, D), MIX_WIDTH ** -0.5),
        'g_post_mix': 1.0 + nrm(ks[19], (L, D), 0.02),
        'g_pre_ffn': 1.0 + nrm(ks[20], (L, D), 0.02),
        'w_up': nrm(ks[21], (L, D, 2 * D_FF), D ** -0.5),
        'conv_ffn': nrm(ks[22], (L, FFN_CONV, 2 * D_FF), FFN_CONV ** -0.5),
        'b_conv_ffn': nrm(ks[23], (L, 2 * D_FF), 0.02),
        'w_down': nrm(ks[24], (L, D_FF, D), D_FF ** -0.5),
        'g_post_ffn': 1.0 + nrm(ks[25], (L, D), 0.02),
    }


def reference(x_prompt, x_sample, c_prompt, c_sample, w_ada, b_ada, g_pre_mix, w_in, conv_a, a_log, dt_bias,
              norm_a, w_gla_up, b_gla, norm_b, bias_ig, bias_fg, norm_c, w_out, g_post_mix, g_pre_ffn, w_up,
              conv_ffn, b_conv_ffn, w_down, g_post_ffn):
    weights = (w_ada, b_ada, g_pre_mix, w_in, conv_a, a_log, dt_bias, norm_a, w_gla_up, b_gla, norm_b,
               bias_ig, bias_fg, norm_c, w_out, g_post_mix, g_pre_ffn, w_up, conv_ffn, b_conv_ffn, w_down,
               g_post_ffn)
    y_prompt = trunk(x_prompt, c_prompt, weights)
    y_sample = trunk(x_sample, c_sample, weights)
    return (y_prompt, y_sample)
```

```python
import functools
import math

import numpy as np
import jax
import jax.numpy as jnp
from jax import lax
from jax.experimental import pallas as pl
from jax.experimental.pallas import tpu as pltpu

F32 = jnp.float32
BF16 = jnp.bfloat16

D_MODEL = 1024
A_HEADS, A_DK, A_DV = 8, 64, 64
B_HEADS, B_DK, B_DV, B_RANK, B_TAU = 4, 32, 64, 16, 16.0
C_HEADS, C_DK, C_DV = 4, 64, 64
D_FF = 2688
EPS = 1e-6

A_QK = A_HEADS * A_DK
A_VW = A_HEADS * A_DV
A_CONV_CH = 2 * A_QK + A_VW
B_QK = B_HEADS * B_DK
B_VW = B_HEADS * B_DV
C_QK = C_HEADS * C_DK
C_VW = C_HEADS * C_DV

LANES = 128
SUBLANES = 8
CHUNK = 64
GROUP = 4
GW = GROUP * 64

W_REST_A = A_VW + LANES
W_B = 2 * B_QK + 2 * B_VW + LANES
W_C = 2 * C_QK + 2 * C_VW + LANES
W_IN_PAD = A_CONV_CH + W_REST_A + W_B + W_C

T_PROJ = 256
T_MIX = 512
T_OUT = 512
T_FFN = 512
F_TILE = 896
VMEM_LIMIT = 56 * 1024 * 1024


def _dot(a, b):
    return jnp.dot(a, b, preferred_element_type=F32)


def _dot_nt(a, b):
    return lax.dot_general(a, b, (((1,), (1,)), ((), ())), preferred_element_type=F32)


def _dot_tn(a, b):
    return lax.dot_general(a, b, (((0,), (0,)), ((), ())), preferred_element_type=F32)


def _bf(x):
    return x.astype(BF16)


def _split3(x):
    hi = _bf(x)
    r = x - hi.astype(F32)
    mid = _bf(r)
    lo = _bf(r - mid.astype(F32))
    return hi, mid, lo


def _split2(x):
    hi = _bf(x)
    lo = _bf(x - hi.astype(F32))
    return hi, lo


def _cdot3(c, x):
    hi, mid, lo = _split3(x)
    return _dot(c, hi) + _dot(c, mid) + _dot(c, lo)


def _dotc3(x, c):
    hi, mid, lo = _split3(x)
    return _dot(hi, c) + _dot(mid, c) + _dot(lo, c)


def _tile_rows(x, n):
    return jnp.concatenate([x] * n, axis=0)


def _silu(x):
    return x * (1.0 / (1.0 + jnp.exp(-x)))


def _sigmoid(x):
    return 1.0 / (1.0 + jnp.exp(-x))


def _softplus(x):
    return jnp.maximum(x, 0.0) + jnp.log(1.0 + jnp.exp(-jnp.abs(x)))


def _log_sigmoid(x):
    return -_softplus(-x)


def _rms(x, g_row):
    return x * lax.rsqrt(jnp.mean(x * x, axis=-1, keepdims=True) + EPS) * g_row


def _params(sem):
    return pltpu.CompilerParams(dimension_semantics=sem, vmem_limit_bytes=VMEM_LIMIT)


def _dir_masks(rev):
    i = np.arange(CHUNK)[:, None]
    j = np.arange(CHUNK)[None, :]
    if rev:
        i, j = CHUNK - 1 - i, CHUNK - 1 - j
    incl = j <= i
    strict = j < i
    rows = [incl, strict, np.eye(CHUNK, dtype=bool), (i // 8 == j // 8) & strict]
    for s in (8, 16, 32, 1, 2, 4):
        rows.append((i // (2 * s) == j // (2 * s)) & (i // s == j // s + 1))
    return np.stack([np.tile(r, (1, GROUP)) for r in rows]).astype(np.float32), incl.astype(np.float32)


M_INCL, M_STRICT, M_EYE, M_BLK8, M_OFF8, M_OFF16, M_OFF32, M_OFF1, M_OFF2, M_OFF4 = range(10)
_OFF_INDEX = {8: M_OFF8, 16: M_OFF16, 32: M_OFF32, 1: M_OFF1, 2: M_OFF2, 4: M_OFF4}


def _gla_cumsum_mats(rev):
    i = np.arange(CHUNK)[:, None]
    t = np.arange(CHUNK)[None, :]
    mats = []
    for s in (32, 16, 8, 4, 2, 1):
        blk = (i // (2 * s)) * 2 * s
        upper = (i % (2 * s)) >= s
        csq = upper & (t >= blk + s) & (t <= i)
        csk = (~upper) & (t >= i + 1) & (t <= blk + s - 1)
        if rev:
            csq, csk = csq[::-1, ::-1], csk[::-1, ::-1]
        mats += [csq, csk]
    return mats


GLA_LEVELS = (32, 16, 8, 4, 2, 1)


def _constants():
    c = {}
    for d, rev in enumerate((False, True)):
        masks, tri = _dir_masks(rev)
        c[f"masks{d}"] = jnp.asarray(masks)
        c[f"tri{d}"] = jnp.asarray(tri, BF16)
        gl = np.concatenate([tri] + [m.astype(np.float32) for m in _gla_cumsum_mats(rev)], axis=0)
        c[f"glacs{d}"] = jnp.asarray(gl, BF16)
    r = np.arange(GW)
    c["bm"] = jnp.asarray((r[:, None] // 64 == r[None, :] // 64), BF16)
    c["bm_f32"] = jnp.asarray((r[:, None] // 64 == r[None, :] // 64), F32)
    c["bmk"] = jnp.asarray((r[:, None] // 64 == np.arange(B_QK)[None, :] // B_DK), BF16)
    c["bmk_f32"] = jnp.asarray((r[:, None] // 64 == np.arange(B_QK)[None, :] // B_DK), F32)
    r5 = np.arange(512)
    c["ones512"] = jnp.asarray((r5[:, None] // 64 == r5[None, :] // 64), BF16)
    ex = np.zeros((LANES, 4 * A_VW), np.float32)
    for q in range(2):
        for d in range(2):
            for h in range(A_HEADS):
                sec = 2 * q + d
                ex[16 * q + 8 * d + h, sec * A_VW + h * 64: sec * A_VW + (h + 1) * 64] = 1.0
    c["exp_a"] = jnp.asarray(ex, BF16)
    ex = np.zeros((LANES, 4 * C_VW), np.float32)
    for q in range(2):
        for d in range(2):
            for h in range(C_HEADS):
                sec = 2 * q + d
                ex[8 * q + 4 * d + h, sec * C_VW + h * 64: sec * C_VW + (h + 1) * 64] = 1.0
    c["exp_c"] = jnp.asarray(ex, BF16)
    return c


def _mod_kernel(c_ref, w_ref, b_ref, o_ref):
    c = c_ref[...]
    s = _silu(c)
    s_hi, s_lo = _split2(s)
    w = w_ref[...]
    w_hi, w_lo = _split2(w)
    o_ref[...] = _dot(s_hi, w_hi) + _dot(s_hi, w_lo) + _dot(s_lo, w_hi) + b_ref[...]


def _mod(c_all, w_ada_l, b_ada_l):
    n, d = c_all.shape
    n6 = w_ada_l.shape[1]
    tn = 1536
    return pl.pallas_call(
        _mod_kernel,
        grid=(n6 // tn,),
        in_specs=[pl.BlockSpec((n, d), lambda j: (0, 0)),
                  pl.BlockSpec((d, tn), lambda j: (0, j)),
                  pl.BlockSpec((1, tn), lambda j: (0, j))],
        out_specs=pl.BlockSpec((n, tn), lambda j: (0, j)),
        out_shape=jax.ShapeDtypeStruct((n, n6), F32),
        compiler_params=_params(("arbitrary",)),
        name="adaln_mod",
    )(c_all, w_ada_l, b_ada_l.reshape(1, n6))


def _seg_sum64(x, ones_bd):
    hi, lo = _split2(x)
    return _dot(hi, ones_bd) + _dot(lo, ones_bd)


def _inproj_kernel(xp_ref, x_ref, xn_ref, mod_ref, g_ref, w_ref, conv_ref, ones_ref, alog_ref, dtb_ref,
                   q_ref, k_ref, v_ref, z_ref, ga_ref, pb_ref, pc_ref, scr_ref):
    i = pl.program_id(1)
    nt = pl.num_programs(1)
    t = x_ref.shape[1]
    mod = mod_ref[0]
    sh1, sc1 = mod[0:1], mod[1:2]
    xe = jnp.concatenate([xp_ref[0], x_ref[0], xn_ref[0]], axis=0)
    h = _bf(_rms(xe, g_ref[...]) * (1.0 + sc1) + sh1)
    pqkv = _dot(h, w_ref[:, 0:A_CONV_CH])
    scr_ref[...] = pqkv
    scr_ref[0:SUBLANES, :] = pqkv[0:SUBLANES] * (i > 0).astype(F32)
    scr_ref[t + SUBLANES:t + 2 * SUBLANES, :] = pqkv[t + SUBLANES:] * (i < nt - 1).astype(F32)
    cw = conv_ref[...]
    cv = (cw[0:1] * scr_ref[pl.ds(SUBLANES - 1, t), :] + cw[1:2] * scr_ref[pl.ds(SUBLANES, t), :]
          + cw[2:3] * scr_ref[pl.ds(SUBLANES + 1, t), :])
    cv = _silu(cv)
    ones_bd = ones_ref[...]
    q = cv[:, 0:A_QK]
    k = cv[:, A_QK:2 * A_QK]
    q_ref[0] = q * lax.rsqrt(_seg_sum64(q * q, ones_bd) + EPS) * (A_DK ** -0.5)
    k_ref[0] = k * lax.rsqrt(_seg_sum64(k * k, ones_bd) + EPS)
    v_ref[0] = cv[:, 2 * A_QK:]
    hm = h[SUBLANES:t + SUBLANES]
    rest = _dot(hm, w_ref[:, A_CONV_CH:])
    z_ref[0] = rest[:, 0:A_VW]
    gt = rest[:, A_VW:W_REST_A]
    col = lax.broadcasted_iota(jnp.int32, gt.shape, 1)
    gdec = -jnp.exp(alog_ref[...]) * _softplus(gt + dtb_ref[...])
    ga_ref[0] = jnp.where(col < 2 * A_HEADS, gdec, _sigmoid(gt))
    pb_ref[0] = rest[:, W_REST_A:W_REST_A + W_B]
    pc_ref[0] = rest[:, W_REST_A + W_B:]


def _inproj(x, mod, g_pre, w_in_p, conv_a, ones512, alog_row, dtb_row):
    b, s, d = x.shape
    t = T_PROJ
    nt = s // t
    tb = t // SUBLANES
    nblk8 = s // SUBLANES
    shapes = [(A_QK, "q"), (A_QK, "k"), (A_VW, "v"), (A_VW, "z"), (LANES, "ga"), (W_B, "pb"), (W_C, "pc")]
    full = lambda shape: pl.BlockSpec(shape, lambda bi, i: (0,) * len(shape))
    return pl.pallas_call(
        _inproj_kernel,
        grid=(b, nt),
        in_specs=[pl.BlockSpec((1, SUBLANES, d), lambda bi, i: (bi, jnp.maximum(i * tb - 1, 0), 0)),
                  pl.BlockSpec((1, t, d), lambda bi, i: (bi, i, 0)),
                  pl.BlockSpec((1, SUBLANES, d), lambda bi, i: (bi, jnp.minimum((i + 1) * tb, nblk8 - 1), 0)),
                  pl.BlockSpec((1, 6, d), lambda bi, i: (bi, 0, 0)),
                  full((1, d)), full((d, W_IN_PAD)), full((3, A_CONV_CH)), full((512, 512)),
                  full((1, LANES)), full((1, LANES))],
        out_specs=[pl.BlockSpec((1, t, w), lambda bi, i: (bi, i, 0)) for w, _ in shapes],
        out_shape=[jax.ShapeDtypeStruct((b, s, w), F32) for w, _ in shapes],
        scratch_shapes=[pltpu.VMEM((t + 2 * SUBLANES, A_CONV_CH), F32)],
        compiler_params=_params(("parallel", "arbitrary")),
        name="inproj",
    )(x, x, x, mod, g_pre, w_in_p, conv_a, ones512, alog_row, dtb_row)


def _bd(x_bf, bm_bf):
    return _tile_rows(x_bf, GROUP) * bm_bf


def _hdot(a, b_bf_bd):
    return _dot(_bf(a), b_bf_bd)


def _chunk_rows(c, n_chunks, rev):
    idx = (n_chunks - 1 - c) if rev else c
    return pl.multiple_of(idx * CHUNK, CHUNK)


def _where_mask(m, x, other):
    return jnp.where(m > 0.5, x, other)


def _otimes3(a, b, bm):
    a_hi, a_lo = _split2(a)
    b_hi, b_lo = _split2(b)
    bd_hi = _bd(b_hi, bm)
    return _dot(a_hi, bd_hi) + _dot(a_lo, bd_hi) + _dot(a_hi, _bd(b_lo, bm))


def _unit_tri_inverse(m, masks, bm):
    d = m * masks[M_BLK8]
    p1 = _otimes3(d, d, bm)
    p2 = _otimes3(p1, p1, bm)
    x = masks[M_EYE] - d
    x = x + _otimes3(x, p1, bm)
    x = x + _otimes3(x, p2, bm)
    for s in (8, 16, 32):
        e = m * masks[_OFF_INDEX[s]]
        x = x - _otimes3(x, _otimes3(e, x, bm), bm)
    return x


def _delta_chunk(q, k, v, g_e, b_e, state, masks, tri, bm, bm_f32, rev):
    gam = _cdot3(tri, g_e)
    gj = jnp.sum(gam * masks[M_EYE], axis=0, keepdims=True)
    decay = jnp.exp(_where_mask(masks[M_INCL], gam - gj, -jnp.inf))
    eg = jnp.exp(gam)
    kb = k * b_e
    kbd = _bd(_bf(k), bm)
    kq = _dot_nt(_bf(jnp.concatenate([kb, q], axis=0)), kbd)
    m = kq[0:CHUNK] * decay * masks[M_STRICT]
    aqk = kq[CHUNK:] * decay
    t_inv = _unit_tri_inverse(m, masks, bm)
    t_hi, t_lo = _split2(t_inv)

    def solve(rhs):
        r_hi, r_lo = _split2(rhs)
        bd_hi = _bd(r_hi, bm)
        return _dot(t_hi, bd_hi) + _dot(t_lo, bd_hi) + _dot(t_hi, _bd(r_lo, bm))

    wv = solve(v * b_e)
    kc = solve(kb * eg)
    last = 0 if rev else CHUNK - 1
    glast = gam[last:last + 1]
    qd = q * eg
    kd = k * jnp.exp(glast - gam)
    cd = jnp.exp(glast)
    s_bf = _bf(state)
    ks_qs = _dot(_bf(jnp.concatenate([kc, qd], axis=0)), s_bf)
    vnew = wv - ks_qs[0:CHUNK]
    vnew_bf = _bf(vnew)
    o = ks_qs[CHUNK:] + _dot(_bf(aqk), _bd(vnew_bf, bm))
    new_state = state * cd + _dot_tn(_bf(kd), vnew_bf) * bm_f32
    return o, new_state


def _mixer_a_kernel(qf_ref, kf_ref, vf_ref, gf_ref, qb_ref, kb_ref, vb_ref, gb_ref,
                    exp_ref, m0_ref, m1_ref, t0_ref, t1_ref, bm_ref, bmf_ref,
                    of_ref, ob_ref, state_ref, ge_ref):
    tb = qf_ref.shape[1]
    n_chunks = tb // CHUNK
    n_groups = A_HEADS // GROUP

    @pl.when(pl.program_id(1) == 0)
    def _():
        state_ref[...] = jnp.zeros_like(state_ref)

    ex = exp_ref[...]
    ge_ref[0] = _dotc3(gf_ref[0], ex)
    ge_ref[1] = _dotc3(gb_ref[0], ex)
    bm = bm_ref[...]
    bm_f32 = bmf_ref[...]

    def body(c, carry):
        for d, (q_ref, k_ref, v_ref, o_ref, m_ref, t_ref) in enumerate(
                ((qf_ref, kf_ref, vf_ref, of_ref, m0_ref, t0_ref),
                 (qb_ref, kb_ref, vb_ref, ob_ref, m1_ref, t1_ref))):
            r0 = _chunk_rows(c, n_chunks, d == 1)
            rows = pl.ds(r0, CHUNK)
            masks = m_ref
            tri = t_ref[...]
            for hg in range(n_groups):
                cols = slice(hg * GW, (hg + 1) * GW)
                g_e = ge_ref[d, rows, d * A_VW + hg * GW: d * A_VW + (hg + 1) * GW]
                b_e = ge_ref[d, rows, (2 + d) * A_VW + hg * GW: (2 + d) * A_VW + (hg + 1) * GW]
                o, st = _delta_chunk(q_ref[0, rows, cols], k_ref[0, rows, cols], v_ref[0, rows, cols],
                                     g_e, b_e, state_ref[d * n_groups + hg], masks, tri, bm, bm_f32, d == 1)
                o_ref[0, rows, cols] = o
                state_ref[d * n_groups + hg] = st
        return carry

    lax.fori_loop(0, n_chunks, body, 0)


def _mixer_a(q, k, v, ga, consts):
    b, s, _ = q.shape
    tb = T_MIX
    nb = s // tb
    fwd = lambda bi, i: (bi, i, 0)
    bwd = lambda bi, i: (bi, nb - 1 - i, 0)
    full = lambda arr: pl.BlockSpec(arr.shape, lambda bi, i: (0,) * arr.ndim)
    blk = lambda w, m: pl.BlockSpec((1, tb, w), m)
    cs = [consts["exp_a"], consts["masks0"], consts["masks1"], consts["tri0"], consts["tri1"],
          consts["bm"], consts["bm_f32"]]
    return pl.pallas_call(
        _mixer_a_kernel,
        grid=(b, nb),
        in_specs=[blk(A_QK, fwd), blk(A_QK, fwd), blk(A_VW, fwd), blk(LANES, fwd),
                  blk(A_QK, bwd), blk(A_QK, bwd), blk(A_VW, bwd), blk(LANES, bwd)] + [full(a) for a in cs],
        out_specs=[blk(A_VW, fwd), blk(A_VW, bwd)],
        out_shape=[jax.ShapeDtypeStruct((b, s, A_VW), F32)] * 2,
        scratch_shapes=[pltpu.VMEM((2 * (A_HEADS // GROUP), GW, GW), F32),
                        pltpu.VMEM((2, tb, 4 * A_VW), F32)],
        compiler_params=_params(("parallel", "arbitrary")),
        name="mixer_delta",
    )(q, k, v, ga, q, k, v, ga, *cs)


def _gla_chunk(q, k, v, g, state_t, masks, glacs, bm, bmk, bmk_f32, rev):
    cs = _cdot3(glacs, g)
    b = cs[0:CHUNK]
    k_bf = _bf(k)
    attn = _dot_nt(_bf(q), _tile_rows(k_bf, GROUP) * bmk) * masks[M_EYE]
    for li, s in enumerate(GLA_LEVELS):
        qe = cs[(1 + 2 * li) * CHUNK:(2 + 2 * li) * CHUNK]
        ke = cs[(2 + 2 * li) * CHUNK:(3 + 2 * li) * CHUNK]
        kl = _tile_rows(_bf(k * jnp.exp(ke)), GROUP) * bmk
        attn = attn + _dot_nt(_bf(q * jnp.exp(qe)), kl) * masks[_OFF_INDEX[s]]
    last = 0 if rev else CHUNK - 1
    blast = b[last:last + 1]
    qd = q * jnp.exp(b)
    kd = k * jnp.exp(blast - b)
    cd = jnp.exp(blast)
    v_bf = _bf(v)
    o = _dot_nt(_bf(qd), _bf(state_t)) + _dot(_bf(attn), _bd(v_bf, bm))
    new_state = state_t * cd + _dot_tn(v_bf, _bf(kd)) * bmk_f32
    return o, new_state


def _mixer_b_kernel(pf_ref, pb_ref, wup_ref, bg_ref, m0_ref, m1_ref, c0_ref, c1_ref, bm_ref, bmk_ref, bmkf_ref,
                    of_ref, ob_ref, state_ref, gk_ref):
    tb = pf_ref.shape[1]
    n_chunks = tb // CHUNK

    @pl.when(pl.program_id(1) == 0)
    def _():
        state_ref[...] = jnp.zeros_like(state_ref)

    lr_col = 2 * B_QK + 2 * B_VW
    for d, p_ref in enumerate((pf_ref, pb_ref)):
        lr = p_ref[0, :, lr_col:lr_col + LANES]
        w = wup_ref[d]
        l_hi, l_lo = _split2(lr)
        w_hi, w_lo = _split2(w)
        zz = _dot(l_hi, w_hi) + _dot(l_hi, w_lo) + _dot(l_lo, w_hi) + bg_ref[d]
        gk_ref[d] = _log_sigmoid(zz) * (1.0 / B_TAU)
    bm = bm_ref[...]
    bmk = bmk_ref[...]
    bmk_f32 = bmkf_ref[...]

    def body(c, carry):
        for d, (p_ref, o_ref, m_ref, c_ref) in enumerate(((pf_ref, of_ref, m0_ref, c0_ref),
                                                          (pb_ref, ob_ref, m1_ref, c1_ref))):
            rows = pl.ds(_chunk_rows(c, n_chunks, d == 1), CHUNK)
            q = p_ref[0, rows, 0:B_QK] * (B_DK ** -0.5)
            k = p_ref[0, rows, B_QK:2 * B_QK]
            v = p_ref[0, rows, 2 * B_QK:2 * B_QK + B_VW]
            o, st = _gla_chunk(q, k, v, gk_ref[d, rows, :], state_ref[d], m_ref, c_ref[...],
                               bm, bmk, bmk_f32, d == 1)
            o_ref[0, rows, :] = o
            state_ref[d] = st
        return carry

    lax.fori_loop(0, n_chunks, body, 0)


def _mixer_b(pb, wup_pad, bgla_rows, consts):
    b, s, _ = pb.shape
    tb = T_MIX
    nb = s // tb
    fwd = lambda bi, i: (bi, i, 0)
    bwd = lambda bi, i: (bi, nb - 1 - i, 0)
    full = lambda arr: pl.BlockSpec(arr.shape, lambda bi, i: (0,) * arr.ndim)
    cs = [wup_pad, bgla_rows, consts["masks0"], consts["masks1"], consts["glacs0"], consts["glacs1"],
          consts["bm"], consts["bmk"], consts["bmk_f32"]]
    return pl.pallas_call(
        _mixer_b_kernel,
        grid=(b, nb),
        in_specs=[pl.BlockSpec((1, tb, W_B), fwd), pl.BlockSpec((1, tb, W_B), bwd)] + [full(a) for a in cs],
        out_specs=[pl.BlockSpec((1, tb, B_VW), fwd), pl.BlockSpec((1, tb, B_VW), bwd)],
        out_shape=[jax.ShapeDtypeStruct((b, s, B_VW), F32)] * 2,
        scratch_shapes=[pltpu.VMEM((2, B_VW, B_QK), F32), pltpu.VMEM((2, tb, B_QK), F32)],
        compiler_params=_params(("parallel", "arbitrary")),
        name="mixer_gla",
    )(pb, pb, *cs)


def _cummax_rows(u, rev):
    n = u.shape[0]
    row = lax.broadcasted_iota(jnp.int32, u.shape, 0)
    x = u
    step = 1
    while step < n:
        if rev:
            sh = jnp.where(row < n - step, pltpu.roll(x, n - step, 0), -jnp.inf)
        else:
            sh = jnp.where(row >= step, pltpu.roll(x, step, 0), -jnp.inf)
        x = jnp.maximum(x, sh)
        step *= 2
    return x


def _mlstm_chunk(q, k, v, ip_e, lf_e, cmat, nrow, mrow, masks, tri, bm, bm_f32, rev):
    fc = _cdot3(tri, lf_e)
    u = ip_e - fc
    cm = _cummax_rows(u, rev)
    dmax = fc + cm
    uj = jnp.sum(u * masks[M_EYE], axis=0, keepdims=True)
    pexp = jnp.exp(_where_mask(masks[M_INCL], uj - cm, -jnp.inf))
    q_bf = _bf(q)
    p = _dot_nt(q_bf, _bd(_bf(k), bm)) * pexp
    last = 0 if rev else CHUNK - 1
    flast = fc[last:last + 1]
    gmax = flast + cm[last:last + 1]
    m_inter = fc + mrow
    m_i = jnp.maximum(dmax, m_inter)
    s_inter = jnp.exp(m_inter - m_i)
    s_intra = jnp.exp(dmax - m_i)
    v_bf = _bf(v)
    p_bf = _bf(p)
    num = s_inter * _dot(q_bf, _bf(cmat)) + s_intra * _dot(p_bf, _bd(v_bf, bm))
    seg = _dot(jnp.concatenate([_bf(q * nrow), p_bf], axis=0), bm)
    den = s_inter * seg[0:CHUNK] + s_intra * seg[CHUNK:]
    h = num / jnp.maximum(jnp.abs(den), jnp.exp(-m_i))
    m_new = jnp.maximum(flast + mrow, gmax)
    a = jnp.exp(flast + mrow - m_new)
    kw = k * jnp.exp(flast + u - m_new)
    new_c = a * cmat + _dot_tn(_bf(kw), v_bf) * bm_f32
    new_n = a * nrow + jnp.sum(kw, axis=0, keepdims=True)
    return h, new_c, new_n, m_new


def _mixer_c_kernel(pf_ref, pb_ref, exp_ref, big_ref, bfg_ref, m0_ref, m1_ref, t0_ref, t1_ref, bm_ref, bmf_ref,
                    of_ref, ob_ref, c_ref, n_ref, mx_ref, ge_ref):
    tb = pf_ref.shape[1]
    n_chunks = tb // CHUNK

    @pl.when(pl.program_id(1) == 0)
    def _():
        c_ref[...] = jnp.zeros_like(c_ref)
        n_ref[...] = jnp.zeros_like(n_ref)
        mx_ref[...] = jnp.zeros_like(mx_ref)

    gate_col = 2 * C_QK + 2 * C_VW
    ex = exp_ref[...]
    for d, p_ref in enumerate((pf_ref, pb_ref)):
        gt = p_ref[0, :, gate_col:gate_col + LANES]
        col = lax.broadcasted_iota(jnp.int32, gt.shape, 1)
        gates = jnp.where(col < 2 * C_HEADS, gt + big_ref[...], _log_sigmoid(gt + bfg_ref[...]))
        ge_ref[d] = _dotc3(gates, ex)
    bm = bm_ref[...]
    bm_f32 = bmf_ref[...]

    def body(c, carry):
        for d, (p_ref, o_ref, m_ref, t_ref) in enumerate(((pf_ref, of_ref, m0_ref, t0_ref),
                                                          (pb_ref, ob_ref, m1_ref, t1_ref))):
            rows = pl.ds(_chunk_rows(c, n_chunks, d == 1), CHUNK)
            q = p_ref[0, rows, 0:C_QK]
            k = p_ref[0, rows, C_QK:2 * C_QK] * (C_DK ** -0.5)
            v = p_ref[0, rows, 2 * C_QK:2 * C_QK + C_VW]
            ip_e = ge_ref[d, rows, d * C_VW:(d + 1) * C_VW]
            lf_e = ge_ref[d, rows, (2 + d) * C_VW:(3 + d) * C_VW]
            h, cn, nn, mn = _mlstm_chunk(q, k, v, ip_e, lf_e, c_ref[d], n_ref[d], mx_ref[d],
                                         m_ref, t_ref[...], bm, bm_f32, d == 1)
            o_ref[0, rows, :] = h
            c_ref[d] = cn
            n_ref[d] = nn
            mx_ref[d] = mn
        return carry

    lax.fori_loop(0, n_chunks, body, 0)


def _mixer_c(pc, big_row, bfg_row, consts):
    b, s, _ = pc.shape
    tb = T_MIX
    nb = s // tb
    fwd = lambda bi, i: (bi, i, 0)
    bwd = lambda bi, i: (bi, nb - 1 - i, 0)
    full = lambda arr: pl.BlockSpec(arr.shape, lambda bi, i: (0,) * arr.ndim)
    cs = [consts["exp_c"], big_row, bfg_row, consts["masks0"], consts["masks1"], consts["tri0"], consts["tri1"],
          consts["bm"], consts["bm_f32"]]
    return pl.pallas_call(
        _mixer_c_kernel,
        grid=(b, nb),
        in_specs=[pl.BlockSpec((1, tb, W_C), fwd), pl.BlockSpec((1, tb, W_C), bwd)] + [full(a) for a in cs],
        out_specs=[pl.BlockSpec((1, tb, C_VW), fwd), pl.BlockSpec((1, tb, C_VW), bwd)],
        out_shape=[jax.ShapeDtypeStruct((b, s, C_VW), F32)] * 2,
        scratch_shapes=[pltpu.VMEM((2, GW, GW), F32), pltpu.VMEM((2, 1, GW), F32), pltpu.VMEM((2, 1, GW), F32),
                        pltpu.VMEM((2, tb, 4 * C_VW), F32)],
        compiler_params=_params(("parallel", "arbitrary")),
        name="mixer_mlstm",
    )(pc, pc, *cs)


def _outproj_kernel(x_ref, mod_ref, af_ref, ab_ref, z_ref, bf_ref, bb_ref, bg_ref, cf_ref, cb_ref, co_ref,
                    na_ref, nb_ref, nc_ref, w_ref, gpost_ref, ones_ref, o_ref):
    ones_bd = ones_ref[...]
    oa = af_ref[0] + ab_ref[0]
    ms = _seg_sum64(oa * oa, ones_bd) * (1.0 / A_DV)
    oa = oa * lax.rsqrt(ms + EPS) * na_ref[...] * _silu(z_ref[0])
    ones_q = ones_bd[0:B_VW, 0:B_VW]
    ob = bf_ref[0] + bb_ref[0]
    ms = _seg_sum64(ob * ob, ones_q) * (1.0 / B_DV)
    ob = ob * lax.rsqrt(ms + EPS) * nb_ref[...] * _silu(bg_ref[0])
    hc = _sigmoid(co_ref[0]) * (cf_ref[0] + cb_ref[0])
    mu = _seg_sum64(hc, ones_q) * (1.0 / C_DV)
    dv = hc - mu
    var = _seg_sum64(dv * dv, ones_q) * (1.0 / C_DV)
    oc = dv * lax.rsqrt(var + EPS) * nc_ref[...]
    y = (_dot(_bf(oa), w_ref[0:A_VW, :]) + _dot(_bf(ob), w_ref[A_VW:A_VW + B_VW, :])
         + _dot(_bf(oc), w_ref[A_VW + B_VW:, :]))
    gt1 = mod_ref[0][2:3]
    o_ref[0] = x_ref[0] + gt1 * _rms(y, gpost_ref[...])


def _outproj(x, mod, oaf, oab, z, obf, obb, pb, ocf, ocb, pc, na_row, nb_row, nc_row, w_out_bf, g_post, ones512):
    b, s, d = x.shape
    t = T_OUT
    tok = lambda w, j=0: pl.BlockSpec((1, t, w), lambda bi, i: (bi, i, j))
    full = lambda arr: pl.BlockSpec(arr.shape, lambda bi, i: (0,) * arr.ndim)
    bg_blk = (2 * B_QK + B_VW) // B_VW
    co_blk = (2 * C_QK + C_VW) // C_VW
    return pl.pallas_call(
        _outproj_kernel,
        grid=(b, s // t),
        in_specs=[tok(d), pl.BlockSpec((1, 6, d), lambda bi, i: (bi, 0, 0)),
                  tok(A_VW), tok(A_VW), tok(A_VW), tok(B_VW), tok(B_VW), tok(B_VW, bg_blk),
                  tok(C_VW), tok(C_VW), tok(C_VW, co_blk),
                  full(na_row), full(nb_row), full(nc_row), full(w_out_bf), full(g_post), full(ones512)],
        out_specs=tok(d),
        out_shape=jax.ShapeDtypeStruct((b, s, d), F32),
        compiler_params=_params(("parallel", "arbitrary")),
        name="outproj",
    )(x, mod, oaf, oab, z, obf, obb, pb, ocf, ocb, pc, na_row, nb_row, nc_row, w_out_bf, g_post, ones512)


def _ffn_kernel(xp_ref, x_ref, xn_ref, mod_ref, gpre_ref, wg_ref, wv_ref, cg_ref, cvw_ref, bg_ref, bv_ref,
                wd_ref, gpost_ref, o_ref, h_ref, u_ref, acc_ref):
    i = pl.program_id(1)
    nt = pl.num_programs(1)
    f = pl.program_id(2)
    nf = pl.num_programs(2)
    t = x_ref.shape[1]
    ft = wg_ref.shape[1]
    mod = mod_ref[0]

    @pl.when(f == 0)
    def _():
        xe = jnp.concatenate([xp_ref[0], x_ref[0], xn_ref[0]], axis=0)
        h_ref[...] = _bf(_rms(xe, gpre_ref[...]) * (1.0 + mod[4:5]) + mod[3:4])
        acc_ref[...] = jnp.zeros_like(acc_ref)

    h = h_ref[...]
    lo_ok = (i > 0).astype(F32)
    hi_ok = (i < nt - 1).astype(F32)

    def conv(w_ref, cw_ref, b_ref, col0):
        u = _dot(h, w_ref[...])
        cols = slice(col0, col0 + ft)
        u_ref[:, cols] = u
        u_ref[0:SUBLANES, cols] = u[0:SUBLANES] * lo_ok
        u_ref[t + SUBLANES:t + 2 * SUBLANES, cols] = u[t + SUBLANES:] * hi_ok
        cw = cw_ref[...]
        return (cw[0:1] * u_ref[pl.ds(SUBLANES - 1, t), cols] + cw[1:2] * u_ref[pl.ds(SUBLANES, t), cols]
                + cw[2:3] * u_ref[pl.ds(SUBLANES + 1, t), cols] + b_ref[...])

    gate = conv(wg_ref, cg_ref, bg_ref, 0)
    val = conv(wv_ref, cvw_ref, bv_ref, ft)
    act = 0.5 * gate * (1.0 + lax.erf(gate * (2.0 ** -0.5))) * val
    acc_ref[...] += _dot(_bf(act), wd_ref[...])

    @pl.when(f == nf - 1)
    def _():
        o_ref[0] = x_ref[0] + mod[5:6] * _rms(acc_ref[...], gpost_ref[...])


def _ffn(x, mod, g_pre, w_up_bf, conv_w, conv_b, w_down_bf, g_post):
    b, s, d = x.shape
    t = T_FFN
    ft = F_TILE
    nf = D_FF // ft
    tb = t // SUBLANES
    nblk8 = s // SUBLANES
    full = lambda shape: pl.BlockSpec(shape, lambda bi, i, f: (0,) * len(shape))
    return pl.pallas_call(
        _ffn_kernel,
        grid=(b, s // t, nf),
        in_specs=[pl.BlockSpec((1, SUBLANES, d), lambda bi, i, f: (bi, jnp.maximum(i * tb - 1, 0), 0)),
                  pl.BlockSpec((1, t, d), lambda bi, i, f: (bi, i, 0)),
                  pl.BlockSpec((1, SUBLANES, d), lambda bi, i, f: (bi, jnp.minimum((i + 1) * tb, nblk8 - 1), 0)),
                  pl.BlockSpec((1, 6, d), lambda bi, i, f: (bi, 0, 0)),
                  full((1, d)),
                  pl.BlockSpec((d, ft), lambda bi, i, f: (0, f)),
                  pl.BlockSpec((d, ft), lambda bi, i, f: (0, nf + f)),
                  pl.BlockSpec((3, ft), lambda bi, i, f: (0, f)),
                  pl.BlockSpec((3, ft), lambda bi, i, f: (0, nf + f)),
                  pl.BlockSpec((1, ft), lambda bi, i, f: (0, f)),
                  pl.BlockSpec((1, ft), lambda bi, i, f: (0, nf + f)),
                  pl.BlockSpec((ft, d), lambda bi, i, f: (f, 0)),
                  full((1, d))],
        out_specs=pl.BlockSpec((1, t, d), lambda bi, i, f: (bi, i, 0)),
        out_shape=jax.ShapeDtypeStruct((b, s, d), F32),
        scratch_shapes=[pltpu.VMEM((t + 2 * SUBLANES, d), BF16),
                        pltpu.VMEM((t + 2 * SUBLANES, 2 * ft), F32),
                        pltpu.VMEM((t, d), F32)],
        compiler_params=_params(("parallel", "arbitrary", "arbitrary")),
        name="ffn",
    )(x, x, x, mod, g_pre, w_up_bf, w_up_bf, conv_w, conv_w, conv_b, conv_b, w_down_bf, g_post)


def _pad_cols(w, sections):
    parts, start = [], 0
    for width, padded in sections:
        parts.append(w[:, start:start + width])
        if padded > width:
            parts.append(jnp.zeros((w.shape[0], padded - width), w.dtype))
        start += width
    return jnp.concatenate(parts, axis=1)


def _row(v, width=None):
    v = v.reshape(1, -1).astype(F32)
    if width is not None and v.shape[1] < width:
        v = jnp.pad(v, ((0, 0), (0, width - v.shape[1])))
    return v


def kernel(x_prompt, x_sample, c_prompt, c_sample, w_ada, b_ada, g_pre_mix, w_in, conv_a, a_log, dt_bias, norm_a,
           w_gla_up, b_gla, norm_b, bias_ig, bias_fg, norm_c, w_out, g_post_mix, g_pre_ffn, w_up, conv_ffn,
           b_conv_ffn, w_down, g_post_ffn):
    consts = _constants()
    depth = w_ada.shape[0]
    d = D_MODEL
    n_prompt = x_prompt.shape[0]
    c_all = jnp.concatenate([c_prompt, c_sample], axis=0)
    xs = [x_prompt, x_sample]
    a_in = A_CONV_CH + A_VW + 4 * A_HEADS
    b_in = 2 * B_QK + 2 * B_VW + 2 * B_RANK
    c_in = 2 * C_QK + 2 * C_VW + 4 * C_HEADS
    for l in range(depth):
        mod_all = _mod(c_all, w_ada[l], b_ada[l])
        mods = [mod_all[:n_prompt].reshape(-1, 6, d), mod_all[n_prompt:].reshape(-1, 6, d)]
        w_in_p = _pad_cols(w_in[l], [(a_in, A_CONV_CH + W_REST_A), (b_in, W_B), (c_in, W_C)]).astype(BF16)
        w_out_bf = w_out[l].astype(BF16)
        w_up_bf = w_up[l].astype(BF16)
        w_down_bf = w_down[l].astype(BF16)
        alog_row = _row(a_log[l], LANES)
        dtb_row = _row(dt_bias[l], LANES)
        wup_pad = jnp.zeros((2, LANES, B_QK), F32)
        for dd in range(2):
            wup_pad = wup_pad.at[dd, dd * B_RANK:(dd + 1) * B_RANK, :].set(w_gla_up[l, dd])
        bgla_rows = b_gla[l].reshape(2, 1, B_QK).astype(F32)
        big_row = _row(bias_ig[l], LANES)
        bfg_row = jnp.pad(bias_fg[l].reshape(1, -1).astype(F32), ((0, 0), (2 * C_HEADS, LANES - 4 * C_HEADS)))
        na_row = _row(jnp.tile(norm_a[l], A_HEADS))
        nb_row = _row(jnp.tile(norm_b[l], B_HEADS))
        nc_row = _row(norm_c[l])
        for gi in range(2):
            x, mod = xs[gi], mods[gi]
            q, k, v, z, ga, pb, pc = _inproj(x, mod, _row(g_pre_mix[l]), w_in_p, conv_a[l], consts["ones512"],
                                             alog_row, dtb_row)
            oaf, oab = _mixer_a(q, k, v, ga, consts)
            obf, obb = _mixer_b(pb, wup_pad, bgla_rows, consts)
            ocf, ocb = _mixer_c(pc, big_row, bfg_row, consts)
            x = _outproj(x, mod, oaf, oab, z, obf, obb, pb, ocf, ocb, pc, na_row, nb_row, nc_row, w_out_bf,
                         _row(g_post_mix[l]), consts["ones512"])
            x = _ffn(x, mod, _row(g_pre_ffn[l]), w_up_bf, conv_ffn[l], _row(b_conv_ffn[l]), w_down_bf,
                     _row(g_post_ffn[l]))
            xs[gi] = x
    return (xs[0], xs[1])
```

```python
import functools
import math

import numpy as np
import jax
import jax.numpy as jnp
from jax import lax
from jax.experimental import pallas as pl
from jax.experimental.pallas import tpu as pltpu

F32 = jnp.float32
BF16 = jnp.bfloat16

D_MODEL = 1024
A_HEADS, A_DK, A_DV = 8, 64, 64
B_HEADS, B_DK, B_DV, B_RANK, B_TAU = 4, 32, 64, 16, 16.0
C_HEADS, C_DK, C_DV = 4, 64, 64
D_FF = 2688
EPS = 1e-6

A_QK = A_HEADS * A_DK
A_VW = A_HEADS * A_DV
A_CONV_CH = 2 * A_QK + A_VW
B_QK = B_HEADS * B_DK
B_VW = B_HEADS * B_DV
C_QK = C_HEADS * C_DK
C_VW = C_HEADS * C_DV

LANES = 128
SUBLANES = 8
CHUNK = 64
GROUP = 4
GW = GROUP * 64

W_REST_A = A_VW + LANES
W_B = 2 * B_QK + 2 * B_VW + LANES
W_C = 2 * C_QK + 2 * C_VW + LANES
W_IN_PAD = A_CONV_CH + W_REST_A + W_B + W_C

T_PROJ = 256
T_MIX = 512
T_OUT = 512
T_FFN = 512
F_TILE = 896
VMEM_LIMIT = 56 * 1024 * 1024
A_CHUNKS_PER_ITER = 4
BC_CHUNKS_PER_ITER = 4


def _dot(a, b):
    return jnp.dot(a, b, preferred_element_type=F32)


def _dot_nt(a, b):
    return lax.dot_general(a, b, (((1,), (1,)), ((), ())), preferred_element_type=F32)


def _dot_tn(a, b):
    return lax.dot_general(a, b, (((0,), (0,)), ((), ())), preferred_element_type=F32)


def _bf(x):
    return x.astype(BF16)


def _split2(x):
    hi = _bf(x)
    lo = _bf(x - hi.astype(F32))
    return hi, lo


def _cdot_hl(c, x):
    hi, lo = _split2(x)
    return _dot(c, hi) + _dot(c, lo)


def _dotc_hl(x, c):
    hi, lo = _split2(x)
    return _dot(hi, c) + _dot(lo, c)


def _tile_rows(x, n):
    return jnp.concatenate([x] * n, axis=0)


def _silu(x):
    return x * (1.0 / (1.0 + jnp.exp(-x)))


def _sigmoid(x):
    return 1.0 / (1.0 + jnp.exp(-x))


def _softplus(x):
    return jnp.maximum(x, 0.0) + jnp.log(1.0 + jnp.exp(-jnp.abs(x)))


def _log_sigmoid(x):
    return -_softplus(-x)


def _rms(x, g_row):
    return x * lax.rsqrt(jnp.mean(x * x, axis=-1, keepdims=True) + EPS) * g_row


def _params(sem):
    return pltpu.CompilerParams(dimension_semantics=sem, vmem_limit_bytes=VMEM_LIMIT)


def _dir_masks(rev):
    i = np.arange(CHUNK)[:, None]
    j = np.arange(CHUNK)[None, :]
    if rev:
        i, j = CHUNK - 1 - i, CHUNK - 1 - j
    incl = j <= i
    strict = j < i
    rows = [incl, strict, np.eye(CHUNK, dtype=bool), (i // 8 == j // 8) & strict]
    for s in (8, 16, 32, 1, 2, 4):
        rows.append((i // (2 * s) == j // (2 * s)) & (i // s == j // s + 1))
    return np.stack([np.tile(r, (1, GROUP)) for r in rows]).astype(np.float32), incl.astype(np.float32)


M_INCL, M_STRICT, M_EYE, M_BLK8, M_OFF8, M_OFF16, M_OFF32, M_OFF1, M_OFF2, M_OFF4 = range(10)
_OFF_INDEX = {8: M_OFF8, 16: M_OFF16, 32: M_OFF32, 1: M_OFF1, 2: M_OFF2, 4: M_OFF4}


def _gla_cumsum_mats(rev):
    i = np.arange(CHUNK)[:, None]
    t = np.arange(CHUNK)[None, :]
    mats = []
    for s in (32, 16, 8, 4, 2, 1):
        blk = (i // (2 * s)) * 2 * s
        upper = (i % (2 * s)) >= s
        csq = upper & (t >= blk + s) & (t <= i)
        csk = (~upper) & (t >= i + 1) & (t <= blk + s - 1)
        if rev:
            csq, csk = csq[::-1, ::-1], csk[::-1, ::-1]
        mats += [csq, csk]
    return mats


GLA_LEVELS = (32, 16, 8, 4, 2, 1)


def _constants():
    c = {}
    for d, rev in enumerate((False, True)):
        masks, tri = _dir_masks(rev)
        c[f"masks{d}"] = jnp.asarray(masks)
        c[f"tri{d}"] = jnp.asarray(tri, BF16)
        gl = np.concatenate([tri] + [m.astype(np.float32) for m in _gla_cumsum_mats(rev)], axis=0)
        c[f"glacs{d}"] = jnp.asarray(gl, BF16)
    r = np.arange(GW)
    c["bm"] = jnp.asarray((r[:, None] // 64 == r[None, :] // 64), BF16)
    c["bm_f32"] = jnp.asarray((r[:, None] // 64 == r[None, :] // 64), F32)
    c["bmk"] = jnp.asarray((r[:, None] // 64 == np.arange(B_QK)[None, :] // B_DK), BF16)
    c["bmk_f32"] = jnp.asarray((r[:, None] // 64 == np.arange(B_QK)[None, :] // B_DK), F32)
    r5 = np.arange(512)
    c["ones512"] = jnp.asarray((r5[:, None] // 64 == r5[None, :] // 64), BF16)
    for d in range(2):
        ex = np.zeros((LANES, 2 * A_VW), np.float32)
        for q in range(2):
            for h in range(A_HEADS):
                ex[16 * q + 8 * d + h, q * A_VW + h * 64: q * A_VW + (h + 1) * 64] = 1.0
        c[f"exp_a{d}"] = jnp.asarray(ex, BF16)
    for d in range(2):
        ex = np.zeros((LANES, 2 * C_VW), np.float32)
        for q in range(2):
            for h in range(C_HEADS):
                ex[8 * q + 4 * d + h, q * C_VW + h * 64: q * C_VW + (h + 1) * 64] = 1.0
        c[f"exp_c{d}"] = jnp.asarray(ex, BF16)
    return c


def _mod_kernel(c_ref, w_ref, b_ref, o_ref):
    c = c_ref[...]
    s = _silu(c)
    s_hi, s_lo = _split2(s)
    w = w_ref[...]
    w_hi, w_lo = _split2(w)
    o_ref[...] = _dot(s_hi, w_hi) + _dot(s_hi, w_lo) + _dot(s_lo, w_hi) + b_ref[...]


def _mod(c_all, w_ada_l, b_ada_l):
    n, d = c_all.shape
    n6 = w_ada_l.shape[1]
    tn = 1536
    return pl.pallas_call(
        _mod_kernel,
        grid=(n6 // tn,),
        in_specs=[pl.BlockSpec((n, d), lambda j: (0, 0)),
                  pl.BlockSpec((d, tn), lambda j: (0, j)),
                  pl.BlockSpec((1, tn), lambda j: (0, j))],
        out_specs=pl.BlockSpec((n, tn), lambda j: (0, j)),
        out_shape=jax.ShapeDtypeStruct((n, n6), F32),
        compiler_params=_params(("arbitrary",)),
        name="adaln_mod",
    )(c_all, w_ada_l, b_ada_l.reshape(1, n6))


def _seg_sum64(x, ones_bd):
    hi, lo = _split2(x)
    return _dot(hi, ones_bd) + _dot(lo, ones_bd)


def _inproj_kernel(xp_ref, x_ref, xn_ref, mod_ref, g_ref, w_ref, conv_ref, ones_ref, alog_ref, dtb_ref,
                   q_ref, k_ref, v_ref, z_ref, ga_ref, pb_ref, pc_ref, scr_ref):
    i = pl.program_id(1)
    nt = pl.num_programs(1)
    t = x_ref.shape[1]
    mod = mod_ref[0]
    sh1, sc1 = mod[0:1], mod[1:2]
    xe = jnp.concatenate([xp_ref[0], x_ref[0], xn_ref[0]], axis=0)
    h = _bf(_rms(xe, g_ref[...]) * (1.0 + sc1) + sh1)
    pqkv = _dot(h, w_ref[:, 0:A_CONV_CH])
    scr_ref[...] = pqkv
    scr_ref[0:SUBLANES, :] = pqkv[0:SUBLANES] * (i > 0).astype(F32)
    scr_ref[t + SUBLANES:t + 2 * SUBLANES, :] = pqkv[t + SUBLANES:] * (i < nt - 1).astype(F32)
    cw = conv_ref[...]
    cv = (cw[0:1] * scr_ref[pl.ds(SUBLANES - 1, t), :] + cw[1:2] * scr_ref[pl.ds(SUBLANES, t), :]
          + cw[2:3] * scr_ref[pl.ds(SUBLANES + 1, t), :])
    cv = _silu(cv)
    ones_bd = ones_ref[...]
    q = cv[:, 0:A_QK]
    k = cv[:, A_QK:2 * A_QK]
    q_ref[0] = q * lax.rsqrt(_seg_sum64(q * q, ones_bd) + EPS) * (A_DK ** -0.5)
    k_ref[0] = k * lax.rsqrt(_seg_sum64(k * k, ones_bd) + EPS)
    v_ref[0] = cv[:, 2 * A_QK:]
    hm = h[SUBLANES:t + SUBLANES]
    rest = _dot(hm, w_ref[:, A_CONV_CH:])
    z_ref[0] = rest[:, 0:A_VW]
    gt = rest[:, A_VW:W_REST_A]
    col = lax.broadcasted_iota(jnp.int32, gt.shape, 1)
    gdec = -jnp.exp(alog_ref[...]) * _softplus(gt + dtb_ref[...])
    ga_ref[0] = jnp.where(col < 2 * A_HEADS, gdec, _sigmoid(gt))
    pb_ref[0] = rest[:, W_REST_A:W_REST_A + W_B]
    pc_ref[0] = rest[:, W_REST_A + W_B:]


def _inproj(x, mod, g_pre, w_in_p, conv_a, ones512, alog_row, dtb_row):
    b, s, d = x.shape
    t = T_PROJ
    nt = s // t
    tb = t // SUBLANES
    nblk8 = s // SUBLANES
    shapes = [(A_QK, "q"), (A_QK, "k"), (A_VW, "v"), (A_VW, "z"), (LANES, "ga"), (W_B, "pb"), (W_C, "pc")]
    full = lambda shape: pl.BlockSpec(shape, lambda bi, i: (0,) * len(shape))
    return pl.pallas_call(
        _inproj_kernel,
        grid=(b, nt),
        in_specs=[pl.BlockSpec((1, SUBLANES, d), lambda bi, i: (bi, jnp.maximum(i * tb - 1, 0), 0)),
                  pl.BlockSpec((1, t, d), lambda bi, i: (bi, i, 0)),
                  pl.BlockSpec((1, SUBLANES, d), lambda bi, i: (bi, jnp.minimum((i + 1) * tb, nblk8 - 1), 0)),
                  pl.BlockSpec((1, 6, d), lambda bi, i: (bi, 0, 0)),
                  full((1, d)), full((d, W_IN_PAD)), full((3, A_CONV_CH)), full((512, 512)),
                  full((1, LANES)), full((1, LANES))],
        out_specs=[pl.BlockSpec((1, t, w), lambda bi, i: (bi, i, 0)) for w, _ in shapes],
        out_shape=[jax.ShapeDtypeStruct((b, s, w), F32) for w, _ in shapes],
        scratch_shapes=[pltpu.VMEM((t + 2 * SUBLANES, A_CONV_CH), F32)],
        compiler_params=_params(("parallel", "arbitrary")),
        name="inproj",
    )(x, x, x, mod, g_pre, w_in_p, conv_a, ones512, alog_row, dtb_row)


def _bd(x_bf, bm_bf):
    return _tile_rows(x_bf, GROUP) * bm_bf


def _hdot(a, b_bf_bd):
    return _dot(_bf(a), b_bf_bd)


def _chunk_rows(c, n_chunks, rev):
    idx = (n_chunks - 1 - c) if rev else c
    return pl.multiple_of(idx * CHUNK, CHUNK)


def _where_mask(m, x, other):
    return jnp.where(m > 0.5, x, other)


def _otimes(a, b, bm):
    return _dot(_bf(a), _bd(_bf(b), bm))


def _delta_local(q, k, v, g_e, b_e, masks, tri, bm, rev):
    gam = _cdot_hl(tri, g_e)
    kb = k * b_e
    kq = _dot_nt(_bf(jnp.concatenate([kb, q], axis=0)), _bd(_bf(k), bm))
    yield
    gj = jnp.sum(gam * masks[M_EYE], axis=0, keepdims=True)
    decay = jnp.exp(_where_mask(masks[M_INCL], gam - gj, -jnp.inf))
    eg = jnp.exp(gam)
    m = kq[0:CHUNK] * decay * masks[M_STRICT]
    aqk = kq[CHUNK:] * decay
    last = 0 if rev else CHUNK - 1
    glast = gam[last:last + 1]
    qd = q * eg
    kd = k * jnp.exp(glast - gam)
    cd = jnp.exp(glast)
    d = m * masks[M_BLK8]
    p1 = _otimes(d, d, bm)
    yield
    p2 = _otimes(p1, p1, bm)
    x = masks[M_EYE] - d
    x = x + _otimes(x, p1, bm)
    yield
    x = x + _otimes(x, p2, bm)
    yield
    for s in (8, 16, 32):
        e = m * masks[_OFF_INDEX[s]]
        ex = _otimes(e, x, bm)
        yield
        x = x - _otimes(x, ex, bm)
        yield
    wv = _otimes(x, v * b_e, bm)
    kc = _otimes(x, kb * eg, bm)
    yield
    return wv, kc, aqk, qd, kd, cd


def _delta_step(local, state_ref, idx, o_ref, rows, cols, bm, bm_f32):
    wv, kc, aqk, qd, kd, cd = local
    state = state_ref[idx]
    ks_qs = _dot(_bf(jnp.concatenate([kc, qd], axis=0)), _bf(state))
    yield
    vnew_bf = _bf(wv - ks_qs[0:CHUNK])
    o_ref[0, rows, cols] = ks_qs[CHUNK:] + _dot(_bf(aqk), _bd(vnew_bf, bm))
    state_ref[idx] = state * cd + _dot_tn(_bf(kd), vnew_bf) * bm_f32
    yield


def _interleave(gens):
    out = [None] * len(gens)
    live = list(enumerate(gens))
    while live:
        nxt = []
        for i, g in live:
            try:
                next(g)
                nxt.append((i, g))
            except StopIteration as stop:
                out[i] = stop.value
        live = nxt
    return out


def _mixer_a_kernel(qf_ref, kf_ref, vf_ref, gf_ref, qb_ref, kb_ref, vb_ref, gb_ref,
                    e0_ref, e1_ref, m0_ref, m1_ref, t0_ref, t1_ref, bm_ref, bmf_ref,
                    of_ref, ob_ref, state_ref, ge_ref):
    tb = qf_ref.shape[1]
    n_chunks = tb // CHUNK
    n_groups = A_HEADS // GROUP

    @pl.when(pl.program_id(1) == 0)
    def _():
        state_ref[...] = jnp.zeros_like(state_ref)

    ge_ref[0] = _dotc_hl(gf_ref[0], e0_ref[...])
    ge_ref[1] = _dotc_hl(gb_ref[0], e1_ref[...])
    bm = bm_ref[...]
    bm_f32 = bmf_ref[...]
    dirs = ((qf_ref, kf_ref, vf_ref, of_ref, m0_ref, t0_ref), (qb_ref, kb_ref, vb_ref, ob_ref, m1_ref, t1_ref))

    def body(it, carry):
        units = []
        for j in range(A_CHUNKS_PER_ITER):
            for d, (q_ref, k_ref, v_ref, o_ref, m_ref, t_ref) in enumerate(dirs):
                rows = pl.ds(_chunk_rows(it * A_CHUNKS_PER_ITER + j, n_chunks, d == 1), CHUNK)
                for hg in range(n_groups):
                    cols = slice(hg * GW, (hg + 1) * GW)
                    units.append((j, d, hg, rows, cols))
        local = _interleave([
            _delta_local(dirs[d][0][0, rows, cols], dirs[d][1][0, rows, cols], dirs[d][2][0, rows, cols],
                         ge_ref[d, rows, cols], ge_ref[d, rows, A_VW + hg * GW:A_VW + (hg + 1) * GW],
                         dirs[d][4], dirs[d][5][...], bm, d == 1)
            for (j, d, hg, rows, cols) in units])
        for j in range(A_CHUNKS_PER_ITER):
            _interleave([_delta_step(local[u], state_ref, d * n_groups + hg, dirs[d][3], rows, cols, bm, bm_f32)
                         for u, (ju, d, hg, rows, cols) in enumerate(units) if ju == j])
        return carry

    lax.fori_loop(0, n_chunks // A_CHUNKS_PER_ITER, body, 0)


def _mixer_a(q, k, v, ga, consts):
    b, s, _ = q.shape
    tb = T_MIX
    nb = s // tb
    fwd = lambda bi, i: (bi, i, 0)
    bwd = lambda bi, i: (bi, nb - 1 - i, 0)
    full = lambda arr: pl.BlockSpec(arr.shape, lambda bi, i: (0,) * arr.ndim)
    blk = lambda w, m: pl.BlockSpec((1, tb, w), m)
    cs = [consts["exp_a0"], consts["exp_a1"], consts["masks0"], consts["masks1"], consts["tri0"], consts["tri1"],
          consts["bm"], consts["bm_f32"]]
    return pl.pallas_call(
        _mixer_a_kernel,
        grid=(b, nb),
        in_specs=[blk(A_QK, fwd), blk(A_QK, fwd), blk(A_VW, fwd), blk(LANES, fwd),
                  blk(A_QK, bwd), blk(A_QK, bwd), blk(A_VW, bwd), blk(LANES, bwd)] + [full(a) for a in cs],
        out_specs=[blk(A_VW, fwd), blk(A_VW, bwd)],
        out_shape=[jax.ShapeDtypeStruct((b, s, A_VW), F32)] * 2,
        scratch_shapes=[pltpu.VMEM((2 * (A_HEADS // GROUP), GW, GW), F32),
                        pltpu.VMEM((2, tb, 2 * A_VW), F32)],
        compiler_params=_params(("parallel", "arbitrary")),
        name="mixer_delta",
    )(q, k, v, ga, q, k, v, ga, *cs)


def _gla_local(q, k, v, g, masks, glacs, bm, bmk, bmk_f32, rev):
    cs = _cdot_hl(glacs, g)
    k_bf = _bf(k)
    attn = _dot_nt(_bf(q), _tile_rows(k_bf, GROUP) * bmk) * masks[M_EYE]
    yield
    b = cs[0:CHUNK]
    for li, s in enumerate(GLA_LEVELS):
        qe = cs[(1 + 2 * li) * CHUNK:(2 + 2 * li) * CHUNK]
        ke = cs[(2 + 2 * li) * CHUNK:(3 + 2 * li) * CHUNK]
        kl = _tile_rows(_bf(k * jnp.exp(ke)), GROUP) * bmk
        attn = attn + _dot_nt(_bf(q * jnp.exp(qe)), kl) * masks[_OFF_INDEX[s]]
        if li % 2 == 1:
            yield
    last = 0 if rev else CHUNK - 1
    blast = b[last:last + 1]
    qd_bf = _bf(q * jnp.exp(b))
    kd = k * jnp.exp(blast - b)
    cd = jnp.exp(blast)
    v_bf = _bf(v)
    o_intra = _dot(_bf(attn), _bd(v_bf, bm))
    upd = _dot_tn(v_bf, _bf(kd)) * bmk_f32
    yield
    return o_intra, qd_bf, upd, cd


def _gla_step(local, state_ref, d, o_ref, rows):
    o_intra, qd_bf, upd, cd = local
    state_t = state_ref[d]
    o_ref[0, rows, :] = o_intra + _dot_nt(qd_bf, _bf(state_t))
    state_ref[d] = state_t * cd + upd
    yield


def _mixer_b_kernel(pf_ref, pb_ref, wup_ref, bg_ref, m0_ref, m1_ref, c0_ref, c1_ref, bm_ref, bmk_ref, bmkf_ref,
                    of_ref, ob_ref, state_ref, gk_ref):
    tb = pf_ref.shape[1]
    n_chunks = tb // CHUNK

    @pl.when(pl.program_id(1) == 0)
    def _():
        state_ref[...] = jnp.zeros_like(state_ref)

    lr_col = 2 * B_QK + 2 * B_VW
    for d, p_ref in enumerate((pf_ref, pb_ref)):
        lr = p_ref[0, :, lr_col:lr_col + LANES]
        w = wup_ref[d]
        l_hi, l_lo = _split2(lr)
        w_hi, w_lo = _split2(w)
        zz = _dot(l_hi, w_hi) + _dot(l_hi, w_lo) + _dot(l_lo, w_hi) + bg_ref[d]
        gk_ref[d] = _log_sigmoid(zz) * (1.0 / B_TAU)
    bm = bm_ref[...]
    bmk = bmk_ref[...]
    bmk_f32 = bmkf_ref[...]

    dirs = ((pf_ref, of_ref, m0_ref, c0_ref), (pb_ref, ob_ref, m1_ref, c1_ref))

    def body(it, carry):
        units = []
        for j in range(BC_CHUNKS_PER_ITER):
            for d in range(2):
                units.append((j, d, pl.ds(_chunk_rows(it * BC_CHUNKS_PER_ITER + j, n_chunks, d == 1), CHUNK)))
        local = _interleave([
            _gla_local(dirs[d][0][0, rows, 0:B_QK] * (B_DK ** -0.5), dirs[d][0][0, rows, B_QK:2 * B_QK],
                       dirs[d][0][0, rows, 2 * B_QK:2 * B_QK + B_VW], gk_ref[d, rows, :],
                       dirs[d][2], dirs[d][3][...], bm, bmk, bmk_f32, d == 1)
            for (j, d, rows) in units])
        for j in range(BC_CHUNKS_PER_ITER):
            _interleave([_gla_step(local[u], state_ref, d, dirs[d][1], rows)
                         for u, (ju, d, rows) in enumerate(units) if ju == j])
        return carry

    lax.fori_loop(0, n_chunks // BC_CHUNKS_PER_ITER, body, 0)


def _mixer_b(pb, wup_pad, bgla_rows, consts):
    b, s, _ = pb.shape
    tb = T_MIX
    nb = s // tb
    fwd = lambda bi, i: (bi, i, 0)
    bwd = lambda bi, i: (bi, nb - 1 - i, 0)
    full = lambda arr: pl.BlockSpec(arr.shape, lambda bi, i: (0,) * arr.ndim)
    cs = [wup_pad, bgla_rows, consts["masks0"], consts["masks1"], consts["glacs0"], consts["glacs1"],
          consts["bm"], consts["bmk"], consts["bmk_f32"]]
    return pl.pallas_call(
        _mixer_b_kernel,
        grid=(b, nb),
        in_specs=[pl.BlockSpec((1, tb, W_B), fwd), pl.BlockSpec((1, tb, W_B), bwd)] + [full(a) for a in cs],
        out_specs=[pl.BlockSpec((1, tb, B_VW), fwd), pl.BlockSpec((1, tb, B_VW), bwd)],
        out_shape=[jax.ShapeDtypeStruct((b, s, B_VW), F32)] * 2,
        scratch_shapes=[pltpu.VMEM((2, B_VW, B_QK), F32), pltpu.VMEM((2, tb, B_QK), F32)],
        compiler_params=_params(("parallel", "arbitrary")),
        name="mixer_gla",
    )(pb, pb, *cs)


def _cummax_rows(u, rev):
    n = u.shape[0]
    row = lax.broadcasted_iota(jnp.int32, u.shape, 0)
    x = u
    step = 1
    while step < n:
        if rev:
            sh = jnp.where(row < n - step, pltpu.roll(x, n - step, 0), -jnp.inf)
        else:
            sh = jnp.where(row >= step, pltpu.roll(x, step, 0), -jnp.inf)
        x = jnp.maximum(x, sh)
        step *= 2
    return x


def _mlstm_local(q, k, v, ip_e, lf_e, masks, tri, bm, bm_f32, rev):
    fc = _cdot_hl(tri, lf_e)
    q_bf = _bf(q)
    qk = _dot_nt(q_bf, _bd(_bf(k), bm))
    yield
    u = ip_e - fc
    cm = _cummax_rows(u, rev)
    dmax = fc + cm
    uj = jnp.sum(u * masks[M_EYE], axis=0, keepdims=True)
    p_bf = _bf(qk * jnp.exp(_where_mask(masks[M_INCL], uj - cm, -jnp.inf)))
    last = 0 if rev else CHUNK - 1
    flast = fc[last:last + 1]
    gmax = flast + cm[last:last + 1]
    v_bf = _bf(v)
    pv = _dot(p_bf, _bd(v_bf, bm))
    psum = _dot(p_bf, bm)
    kg = k * jnp.exp(flast + u - gmax)
    w = _dot_tn(_bf(kg), v_bf) * bm_f32
    ksum = jnp.sum(kg, axis=0, keepdims=True)
    yield
    return q, q_bf, fc, dmax, flast, gmax, pv, psum, w, ksum


def _mlstm_step(local, c_ref, n_ref, mx_ref, d, o_ref, rows, bm):
    q, q_bf, fc, dmax, flast, gmax, pv, psum, w, ksum = local
    cmat, nrow, mrow = c_ref[d], n_ref[d], mx_ref[d]
    qc = _dot(q_bf, _bf(cmat))
    qn = _dot(_bf(q * nrow), bm)
    yield
    m_inter = fc + mrow
    m_i = jnp.maximum(dmax, m_inter)
    s_inter = jnp.exp(m_inter - m_i)
    s_intra = jnp.exp(dmax - m_i)
    num = s_inter * qc + s_intra * pv
    den = s_inter * qn + s_intra * psum
    o_ref[0, rows, :] = num / jnp.maximum(jnp.abs(den), jnp.exp(-m_i))
    m_new = jnp.maximum(flast + mrow, gmax)
    a = jnp.exp(flast + mrow - m_new)
    sc = jnp.exp(gmax - m_new)
    c_ref[d] = a * cmat + sc * w
    n_ref[d] = a * nrow + sc * ksum
    mx_ref[d] = m_new
    yield


def _mixer_c_kernel(pf_ref, pb_ref, e0_ref, e1_ref, big_ref, bfg_ref, m0_ref, m1_ref, t0_ref, t1_ref, bm_ref,
                    bmf_ref, of_ref, ob_ref, c_ref, n_ref, mx_ref, ge_ref):
    tb = pf_ref.shape[1]
    n_chunks = tb // CHUNK

    @pl.when(pl.program_id(1) == 0)
    def _():
        c_ref[...] = jnp.zeros_like(c_ref)
        n_ref[...] = jnp.zeros_like(n_ref)
        mx_ref[...] = jnp.zeros_like(mx_ref)

    gate_col = 2 * C_QK + 2 * C_VW
    for d, (p_ref, e_ref) in enumerate(((pf_ref, e0_ref), (pb_ref, e1_ref))):
        gt = p_ref[0, :, gate_col:gate_col + LANES]
        col = lax.broadcasted_iota(jnp.int32, gt.shape, 1)
        gates = jnp.where(col < 2 * C_HEADS, gt + big_ref[...], _log_sigmoid(gt + bfg_ref[...]))
        ge_ref[d] = _dotc_hl(gates, e_ref[...])
    bm = bm_ref[...]
    bm_f32 = bmf_ref[...]
    dirs = ((pf_ref, of_ref, m0_ref, t0_ref), (pb_ref, ob_ref, m1_ref, t1_ref))

    def body(it, carry):
        units = []
        for j in range(BC_CHUNKS_PER_ITER):
            for d in range(2):
                units.append((j, d, pl.ds(_chunk_rows(it * BC_CHUNKS_PER_ITER + j, n_chunks, d == 1), CHUNK)))
        local = _interleave([
            _mlstm_local(dirs[d][0][0, rows, 0:C_QK], dirs[d][0][0, rows, C_QK:2 * C_QK] * (C_DK ** -0.5),
                         dirs[d][0][0, rows, 2 * C_QK:2 * C_QK + C_VW],
                         ge_ref[d, rows, 0:C_VW], ge_ref[d, rows, C_VW:2 * C_VW],
                         dirs[d][2], dirs[d][3][...], bm, bm_f32, d == 1)
            for (j, d, rows) in units])
        for j in range(BC_CHUNKS_PER_ITER):
            _interleave([_mlstm_step(local[u], c_ref, n_ref, mx_ref, d, dirs[d][1], rows, bm)
                         for u, (ju, d, rows) in enumerate(units) if ju == j])
        return carry

    lax.fori_loop(0, n_chunks // BC_CHUNKS_PER_ITER, body, 0)


def _mixer_c(pc, big_row, bfg_row, consts):
    b, s, _ = pc.shape
    tb = T_MIX
    nb = s // tb
    fwd = lambda bi, i: (bi, i, 0)
    bwd = lambda bi, i: (bi, nb - 1 - i, 0)
    full = lambda arr: pl.BlockSpec(arr.shape, lambda bi, i: (0,) * arr.ndim)
    cs = [consts["exp_c0"], consts["exp_c1"], big_row, bfg_row, consts["masks0"], consts["masks1"],
          consts["tri0"], consts["tri1"], consts["bm"], consts["bm_f32"]]
    return pl.pallas_call(
        _mixer_c_kernel,
        grid=(b, nb),
        in_specs=[pl.BlockSpec((1, tb, W_C), fwd), pl.BlockSpec((1, tb, W_C), bwd)] + [full(a) for a in cs],
        out_specs=[pl.BlockSpec((1, tb, C_VW), fwd), pl.BlockSpec((1, tb, C_VW), bwd)],
        out_shape=[jax.ShapeDtypeStruct((b, s, C_VW), F32)] * 2,
        scratch_shapes=[pltpu.VMEM((2, GW, GW), F32), pltpu.VMEM((2, 1, GW), F32), pltpu.VMEM((2, 1, GW), F32),
                        pltpu.VMEM((2, tb, 2 * C_VW), F32)],
        compiler_params=_params(("parallel", "arbitrary")),
        name="mixer_mlstm",
    )(pc, pc, *cs)


def _outproj_kernel(x_ref, mod_ref, af_ref, ab_ref, z_ref, bf_ref, bb_ref, bg_ref, cf_ref, cb_ref, co_ref,
                    na_ref, nb_ref, nc_ref, w_ref, gpost_ref, ones_ref, o_ref):
    ones_bd = ones_ref[...]
    oa = af_ref[0] + ab_ref[0]
    ms = _seg_sum64(oa * oa, ones_bd) * (1.0 / A_DV)
    oa = oa * lax.rsqrt(ms + EPS) * na_ref[...] * _silu(z_ref[0])
    ones_q = ones_bd[0:B_VW, 0:B_VW]
    ob = bf_ref[0] + bb_ref[0]
    ms = _seg_sum64(ob * ob, ones_q) * (1.0 / B_DV)
    ob = ob * lax.rsqrt(ms + EPS) * nb_ref[...] * _silu(bg_ref[0])
    hc = _sigmoid(co_ref[0]) * (cf_ref[0] + cb_ref[0])
    mu = _seg_sum64(hc, ones_q) * (1.0 / C_DV)
    dv = hc - mu
    var = _seg_sum64(dv * dv, ones_q) * (1.0 / C_DV)
    oc = dv * lax.rsqrt(var + EPS) * nc_ref[...]
    y = (_dot(_bf(oa), w_ref[0:A_VW, :]) + _dot(_bf(ob), w_ref[A_VW:A_VW + B_VW, :])
         + _dot(_bf(oc), w_ref[A_VW + B_VW:, :]))
    gt1 = mod_ref[0][2:3]
    o_ref[0] = x_ref[0] + gt1 * _rms(y, gpost_ref[...])


def _outproj(x, mod, oaf, oab, z, obf, obb, pb, ocf, ocb, pc, na_row, nb_row, nc_row, w_out_bf, g_post, ones512):
    b, s, d = x.shape
    t = T_OUT
    tok = lambda w, j=0: pl.BlockSpec((1, t, w), lambda bi, i: (bi, i, j))
    full = lambda arr: pl.BlockSpec(arr.shape, lambda bi, i: (0,) * arr.ndim)
    bg_blk = (2 * B_QK + B_VW) // B_VW
    co_blk = (2 * C_QK + C_VW) // C_VW
    return pl.pallas_call(
        _outproj_kernel,
        grid=(b, s // t),
        in_specs=[tok(d), pl.BlockSpec((1, 6, d), lambda bi, i: (bi, 0, 0)),
                  tok(A_VW), tok(A_VW), tok(A_VW), tok(B_VW), tok(B_VW), tok(B_VW, bg_blk),
                  tok(C_VW), tok(C_VW), tok(C_VW, co_blk),
                  full(na_row), full(nb_row), full(nc_row), full(w_out_bf), full(g_post), full(ones512)],
        out_specs=tok(d),
        out_shape=jax.ShapeDtypeStruct((b, s, d), F32),
        compiler_params=_params(("parallel", "arbitrary")),
        name="outproj",
    )(x, mod, oaf, oab, z, obf, obb, pb, ocf, ocb, pc, na_row, nb_row, nc_row, w_out_bf, g_post, ones512)


def _ffn_kernel(xp_ref, x_ref, xn_ref, mod_ref, gpre_ref, wg_ref, wv_ref, cg_ref, cvw_ref, bg_ref, bv_ref,
                wd_ref, gpost_ref, o_ref, h_ref, u_ref, acc_ref):
    i = pl.program_id(1)
    nt = pl.num_programs(1)
    f = pl.program_id(2)
    nf = pl.num_programs(2)
    t = x_ref.shape[1]
    ft = wg_ref.shape[1]
    mod = mod_ref[0]

    @pl.when(f == 0)
    def _():
        xe = jnp.concatenate([xp_ref[0], x_ref[0], xn_ref[0]], axis=0)
        h_ref[...] = _bf(_rms(xe, gpre_ref[...]) * (1.0 + mod[4:5]) + mod[3:4])
        acc_ref[...] = jnp.zeros_like(acc_ref)

    h = h_ref[...]
    lo_ok = (i > 0).astype(F32)
    hi_ok = (i < nt - 1).astype(F32)

    def conv(w_ref, cw_ref, b_ref, col0):
        u = _dot(h, w_ref[...])
        cols = slice(col0, col0 + ft)
        u_ref[:, cols] = u
        u_ref[0:SUBLANES, cols] = u[0:SUBLANES] * lo_ok
        u_ref[t + SUBLANES:t + 2 * SUBLANES, cols] = u[t + SUBLANES:] * hi_ok
        cw = cw_ref[...]
        return (cw[0:1] * u_ref[pl.ds(SUBLANES - 1, t), cols] + cw[1:2] * u_ref[pl.ds(SUBLANES, t), cols]
                + cw[2:3] * u_ref[pl.ds(SUBLANES + 1, t), cols] + b_ref[...])

    gate = conv(wg_ref, cg_ref, bg_ref, 0)
    val = conv(wv_ref, cvw_ref, bv_ref, ft)
    act = 0.5 * gate * (1.0 + lax.erf(gate * (2.0 ** -0.5))) * val
    acc_ref[...] += _dot(_bf(act), wd_ref[...])

    @pl.when(f == nf - 1)
    def _():
        o_ref[0] = x_ref[0] + mod[5:6] * _rms(acc_ref[...], gpost_ref[...])


def _ffn(x, mod, g_pre, w_up_bf, conv_w, conv_b, w_down_bf, g_post):
    b, s, d = x.shape
    t = T_FFN
    ft = F_TILE
    nf = D_FF // ft
    tb = t // SUBLANES
    nblk8 = s // SUBLANES
    full = lambda shape: pl.BlockSpec(shape, lambda bi, i, f: (0,) * len(shape))
    return pl.pallas_call(
        _ffn_kernel,
        grid=(b, s // t, nf),
        in_specs=[pl.BlockSpec((1, SUBLANES, d), lambda bi, i, f: (bi, jnp.maximum(i * tb - 1, 0), 0)),
                  pl.BlockSpec((1, t, d), lambda bi, i, f: (bi, i, 0)),
                  pl.BlockSpec((1, SUBLANES, d), lambda bi, i, f: (bi, jnp.minimum((i + 1) * tb, nblk8 - 1), 0)),
                  pl.BlockSpec((1, 6, d), lambda bi, i, f: (bi, 0, 0)),
                  full((1, d)),
                  pl.BlockSpec((d, ft), lambda bi, i, f: (0, f)),
                  pl.BlockSpec((d, ft), lambda bi, i, f: (0, nf + f)),
                  pl.BlockSpec((3, ft), lambda bi, i, f: (0, f)),
                  pl.BlockSpec((3, ft), lambda bi, i, f: (0, nf + f)),
                  pl.BlockSpec((1, ft), lambda bi, i, f: (0, f)),
                  pl.BlockSpec((1, ft), lambda bi, i, f: (0, nf + f)),
                  pl.BlockSpec((ft, d), lambda bi, i, f: (f, 0)),
                  full((1, d))],
        out_specs=pl.BlockSpec((1, t, d), lambda bi, i, f: (bi, i, 0)),
        out_shape=jax.ShapeDtypeStruct((b, s, d), F32),
        scratch_shapes=[pltpu.VMEM((t + 2 * SUBLANES, d), BF16),
                        pltpu.VMEM((t + 2 * SUBLANES, 2 * ft), F32),
                        pltpu.VMEM((t, d), F32)],
        compiler_params=_params(("parallel", "arbitrary", "arbitrary")),
        name="ffn",
    )(x, x, x, mod, g_pre, w_up_bf, w_up_bf, conv_w, conv_w, conv_b, conv_b, w_down_bf, g_post)


def _pad_cols(w, sections):
    parts, start = [], 0
    for width, padded in sections:
        parts.append(w[:, start:start + width])
        if padded > width:
            parts.append(jnp.zeros((w.shape[0], padded - width), w.dtype))
        start += width
    return jnp.concatenate(parts, axis=1)


def _row(v, width=None):
    v = v.reshape(1, -1).astype(F32)
    if width is not None and v.shape[1] < width:
        v = jnp.pad(v, ((0, 0), (0, width - v.shape[1])))
    return v


def kernel(x_prompt, x_sample, c_prompt, c_sample, w_ada, b_ada, g_pre_mix, w_in, conv_a, a_log, dt_bias, norm_a,
           w_gla_up, b_gla, norm_b, bias_ig, bias_fg, norm_c, w_out, g_post_mix, g_pre_ffn, w_up, conv_ffn,
           b_conv_ffn, w_down, g_post_ffn):
    consts = _constants()
    depth = w_ada.shape[0]
    d = D_MODEL
    n_prompt = x_prompt.shape[0]
    c_all = jnp.concatenate([c_prompt, c_sample], axis=0)
    xs = [x_prompt, x_sample]
    a_in = A_CONV_CH + A_VW + 4 * A_HEADS
    b_in = 2 * B_QK + 2 * B_VW + 2 * B_RANK
    c_in = 2 * C_QK + 2 * C_VW + 4 * C_HEADS
    for l in range(depth):
        mod_all = _mod(c_all, w_ada[l], b_ada[l])
        mods = [mod_all[:n_prompt].reshape(-1, 6, d), mod_all[n_prompt:].reshape(-1, 6, d)]
        w_in_p = _pad_cols(w_in[l], [(a_in, A_CONV_CH + W_REST_A), (b_in, W_B), (c_in, W_C)]).astype(BF16)
        w_out_bf = w_out[l].astype(BF16)
        w_up_bf = w_up[l].astype(BF16)
        w_down_bf = w_down[l].astype(BF16)
        alog_row = _row(a_log[l], LANES)
        dtb_row = _row(dt_bias[l], LANES)
        wup_pad = jnp.zeros((2, LANES, B_QK), F32)
        for dd in range(2):
            wup_pad = wup_pad.at[dd, dd * B_RANK:(dd + 1) * B_RANK, :].set(w_gla_up[l, dd])
        bgla_rows = b_gla[l].reshape(2, 1, B_QK).astype(F32)
        big_row = _row(bias_ig[l], LANES)
        bfg_row = jnp.pad(bias_fg[l].reshape(1, -1).astype(F32), ((0, 0), (2 * C_HEADS, LANES - 4 * C_HEADS)))
        na_row = _row(jnp.tile(norm_a[l], A_HEADS))
        nb_row = _row(jnp.tile(norm_b[l], B_HEADS))
        nc_row = _row(norm_c[l])
        for gi in range(2):
            x, mod = xs[gi], mods[gi]
            q, k, v, z, ga, pb, pc = _inproj(x, mod, _row(g_pre_mix[l]), w_in_p, conv_a[l], consts["ones512"],
                                             alog_row, dtb_row)
            oaf, oab = _mixer_a(q, k, v, ga, consts)
            obf, obb = _mixer_b(pb, wup_pad, bgla_rows, consts)
            ocf, ocb = _mixer_c(pc, big_row, bfg_row, consts)
            x = _outproj(x, mod, oaf, oab, z, obf, obb, pb, ocf, ocb, pc, na_row, nb_row, nc_row, w_out_bf,
                         _row(g_post_mix[l]), consts["ones512"])
            x = _ffn(x, mod, _row(g_pre_ffn[l]), w_up_bf, conv_ffn[l], _row(b_conv_ffn[l]), w_down_bf,
                     _row(g_post_ffn[l]))
            xs[gi] = x
    return (xs[0], xs[1])
```

```python
import functools
import math

import numpy as np
import jax
import jax.numpy as jnp
from jax import lax
from jax.experimental import pallas as pl
from jax.experimental.pallas import tpu as pltpu

F32 = jnp.float32
BF16 = jnp.bfloat16

D_MODEL = 1024
A_HEADS, A_DK, A_DV = 8, 64, 64
B_HEADS, B_DK, B_DV, B_RANK, B_TAU = 4, 32, 64, 16, 16.0
C_HEADS, C_DK, C_DV = 4, 64, 64
D_FF = 2688
EPS = 1e-6

A_QK = A_HEADS * A_DK
A_VW = A_HEADS * A_DV
A_CONV_CH = 2 * A_QK + A_VW
B_QK = B_HEADS * B_DK
B_VW = B_HEADS * B_DV
C_QK = C_HEADS * C_DK
C_VW = C_HEADS * C_DV

LANES = 128
SUBLANES = 8
CHUNK = 64
GROUP = 4
GW = GROUP * 64

W_REST_A = A_VW + LANES
W_B = 2 * B_QK + 2 * B_VW + LANES
W_C = 2 * C_QK + 2 * C_VW + LANES
W_IN_PAD = A_CONV_CH + W_REST_A + W_B + W_C

T_PROJ = 512
T_MIX = 512
T_OUT = 512
T_FFN = 512
F_TILE = 896
FFN_LAG = 4
VMEM_LIMIT = 56 * 1024 * 1024
A_CHUNKS_PER_ITER = 4
BC_CHUNKS_PER_ITER = 4


def _dot(a, b):
    return jnp.dot(a, b, preferred_element_type=F32)


def _dot_nt(a, b):
    return lax.dot_general(a, b, (((1,), (1,)), ((), ())), preferred_element_type=F32)


def _dot_tn(a, b):
    return lax.dot_general(a, b, (((0,), (0,)), ((), ())), preferred_element_type=F32)


def _bf(x):
    return x.astype(BF16)


def _split2(x):
    hi = _bf(x)
    lo = _bf(x - hi.astype(F32))
    return hi, lo


def _cdot_hl(c, x):
    hi, lo = _split2(x)
    return _dot(c, hi) + _dot(c, lo)


def _dotc_hl(x, c):
    hi, lo = _split2(x)
    return _dot(hi, c) + _dot(lo, c)


def _tile_rows(x, n):
    return jnp.concatenate([x] * n, axis=0)


def _silu(x):
    return x * (1.0 / (1.0 + jnp.exp(-x)))


def _sigmoid(x):
    return 1.0 / (1.0 + jnp.exp(-x))


def _softplus(x):
    return jnp.maximum(x, 0.0) + jnp.log(1.0 + jnp.exp(-jnp.abs(x)))


def _log_sigmoid(x):
    return -_softplus(-x)


def _rms(x, g_row):
    return x * lax.rsqrt(jnp.mean(x * x, axis=-1, keepdims=True) + EPS) * g_row


def _params(sem):
    return pltpu.CompilerParams(dimension_semantics=sem, vmem_limit_bytes=VMEM_LIMIT)


def _dir_masks(rev):
    i = np.arange(CHUNK)[:, None]
    j = np.arange(CHUNK)[None, :]
    if rev:
        i, j = CHUNK - 1 - i, CHUNK - 1 - j
    incl = j <= i
    strict = j < i
    rows = [incl, strict, np.eye(CHUNK, dtype=bool), (i // 8 == j // 8) & strict]
    for s in (8, 16, 32, 1, 2, 4):
        rows.append((i // (2 * s) == j // (2 * s)) & (i // s == j // s + 1))
    return np.stack([np.tile(r, (1, GROUP)) for r in rows]).astype(np.float32), incl.astype(np.float32)


M_INCL, M_STRICT, M_EYE, M_BLK8, M_OFF8, M_OFF16, M_OFF32, M_OFF1, M_OFF2, M_OFF4 = range(10)
_OFF_INDEX = {8: M_OFF8, 16: M_OFF16, 32: M_OFF32, 1: M_OFF1, 2: M_OFF2, 4: M_OFF4}


def _gla_cumsum_mats(rev):
    i = np.arange(CHUNK)[:, None]
    t = np.arange(CHUNK)[None, :]
    mats = []
    for s in (32, 16, 8, 4, 2, 1):
        blk = (i // (2 * s)) * 2 * s
        upper = (i % (2 * s)) >= s
        csq = upper & (t >= blk + s) & (t <= i)
        csk = (~upper) & (t >= i + 1) & (t <= blk + s - 1)
        if rev:
            csq, csk = csq[::-1, ::-1], csk[::-1, ::-1]
        mats += [csq, csk]
    return mats


GLA_LEVELS = (32, 16, 8, 4, 2, 1)


def _constants():
    c = {}
    for d, rev in enumerate((False, True)):
        masks, tri = _dir_masks(rev)
        c[f"masks{d}"] = jnp.asarray(masks)
        c[f"tri{d}"] = jnp.asarray(tri, BF16)
        gl = np.concatenate([tri] + [m.astype(np.float32) for m in _gla_cumsum_mats(rev)], axis=0)
        c[f"glacs{d}"] = jnp.asarray(gl, BF16)
    r = np.arange(GW)
    c["bm"] = jnp.asarray((r[:, None] // 64 == r[None, :] // 64), BF16)
    c["bm_f32"] = jnp.asarray((r[:, None] // 64 == r[None, :] // 64), F32)
    c["bmk"] = jnp.asarray((r[:, None] // 64 == np.arange(B_QK)[None, :] // B_DK), BF16)
    c["bmk_f32"] = jnp.asarray((r[:, None] // 64 == np.arange(B_QK)[None, :] // B_DK), F32)
    r5 = np.arange(512)
    c["ones512"] = jnp.asarray((r5[:, None] // 64 == r5[None, :] // 64), BF16)
    for d in range(2):
        ex = np.zeros((LANES, 2 * A_VW), np.float32)
        for q in range(2):
            for h in range(A_HEADS):
                ex[16 * q + 8 * d + h, q * A_VW + h * 64: q * A_VW + (h + 1) * 64] = 1.0
        c[f"exp_a{d}"] = jnp.asarray(ex, BF16)
    for d in range(2):
        ex = np.zeros((LANES, 2 * C_VW), np.float32)
        for q in range(2):
            for h in range(C_HEADS):
                ex[8 * q + 4 * d + h, q * C_VW + h * 64: q * C_VW + (h + 1) * 64] = 1.0
        c[f"exp_c{d}"] = jnp.asarray(ex, BF16)
    return c


def _mod_kernel(c_ref, w_ref, b_ref, o_ref):
    c = c_ref[...]
    s = _silu(c)
    s_hi, s_lo = _split2(s)
    w = w_ref[...]
    w_hi, w_lo = _split2(w)
    o_ref[...] = _dot(s_hi, w_hi) + _dot(s_hi, w_lo) + _dot(s_lo, w_hi) + b_ref[...]


def _mod(c_all, w_ada_l, b_ada_l):
    n, d = c_all.shape
    n6 = w_ada_l.shape[1]
    tn = 1536
    return pl.pallas_call(
        _mod_kernel,
        grid=(n6 // tn,),
        in_specs=[pl.BlockSpec((n, d), lambda j: (0, 0)),
                  pl.BlockSpec((d, tn), lambda j: (0, j)),
                  pl.BlockSpec((1, tn), lambda j: (0, j))],
        out_specs=pl.BlockSpec((n, tn), lambda j: (0, j)),
        out_shape=jax.ShapeDtypeStruct((n, n6), F32),
        compiler_params=_params(("arbitrary",)),
        name="adaln_mod",
    )(c_all, w_ada_l, b_ada_l.reshape(1, n6))


def _seg_sum64(x, ones_bd):
    return _dot(_bf(x), ones_bd)


def _inproj_kernel(xp_ref, x_ref, xn_ref, mod_ref, g_ref, w_ref, conv_ref, ones_ref, alog_ref, dtb_ref,
                   q_ref, k_ref, v_ref, z_ref, ga_ref, pb_ref, pc_ref, scr_ref):
    i = pl.program_id(1)
    nt = pl.num_programs(1)
    t = x_ref.shape[1]
    mod = mod_ref[0]
    sh1, sc1 = mod[0:1], mod[1:2]
    xe = jnp.concatenate([xp_ref[0], x_ref[0], xn_ref[0]], axis=0)
    h = _bf(_rms(xe, g_ref[...]) * (1.0 + sc1) + sh1)
    lo_ok = (i > 0).astype(F32)
    hi_ok = (i < nt - 1).astype(F32)
    for part in range(A_CONV_CH // A_QK):
        cols = slice(part * A_QK, (part + 1) * A_QK)
        p = _dot(h, w_ref[:, cols])
        scr_ref[0:SUBLANES, cols] = p[0:SUBLANES] * lo_ok
        scr_ref[SUBLANES:t + SUBLANES, cols] = p[SUBLANES:t + SUBLANES]
        scr_ref[t + SUBLANES:t + 2 * SUBLANES, cols] = p[t + SUBLANES:] * hi_ok
    hm = h[SUBLANES:t + SUBLANES]
    rest = _dot(hm, w_ref[:, A_CONV_CH:])
    ones_bd = ones_ref[...]

    def conv_silu(cols):
        cw = conv_ref[:, cols]
        return _silu(cw[0:1] * scr_ref[pl.ds(SUBLANES - 1, t), cols] + cw[1:2] * scr_ref[pl.ds(SUBLANES, t), cols]
                     + cw[2:3] * scr_ref[pl.ds(SUBLANES + 1, t), cols])

    q = conv_silu(slice(0, A_QK))
    q_ref[0] = q * lax.rsqrt(_seg_sum64(q * q, ones_bd) + EPS) * (A_DK ** -0.5)
    k = conv_silu(slice(A_QK, 2 * A_QK))
    k_ref[0] = k * lax.rsqrt(_seg_sum64(k * k, ones_bd) + EPS)
    v_ref[0] = conv_silu(slice(2 * A_QK, A_CONV_CH))
    z_ref[0] = rest[:, 0:A_VW]
    gt = rest[:, A_VW:W_REST_A]
    col = lax.broadcasted_iota(jnp.int32, gt.shape, 1)
    gdec = -jnp.exp(alog_ref[...]) * _softplus(gt + dtb_ref[...])
    ga_ref[0] = jnp.where(col < 2 * A_HEADS, gdec, _sigmoid(gt))
    pb_ref[0] = rest[:, W_REST_A:W_REST_A + W_B]
    pc_ref[0] = rest[:, W_REST_A + W_B:]


def _inproj(x, mod, g_pre, w_in_p, conv_a, ones512, alog_row, dtb_row):
    b, s, d = x.shape
    t = T_PROJ
    nt = s // t
    tb = t // SUBLANES
    nblk8 = s // SUBLANES
    shapes = [(A_QK, "q"), (A_QK, "k"), (A_VW, "v"), (A_VW, "z"), (LANES, "ga"), (W_B, "pb"), (W_C, "pc")]
    full = lambda shape: pl.BlockSpec(shape, lambda bi, i: (0,) * len(shape))
    return pl.pallas_call(
        _inproj_kernel,
        grid=(b, nt),
        in_specs=[pl.BlockSpec((1, SUBLANES, d), lambda bi, i: (bi, jnp.maximum(i * tb - 1, 0), 0)),
                  pl.BlockSpec((1, t, d), lambda bi, i: (bi, i, 0)),
                  pl.BlockSpec((1, SUBLANES, d), lambda bi, i: (bi, jnp.minimum((i + 1) * tb, nblk8 - 1), 0)),
                  pl.BlockSpec((1, 6, d), lambda bi, i: (bi, 0, 0)),
                  full((1, d)), full((d, W_IN_PAD)), full((3, A_CONV_CH)), full((512, 512)),
                  full((1, LANES)), full((1, LANES))],
        out_specs=[pl.BlockSpec((1, t, w), lambda bi, i: (bi, i, 0)) for w, _ in shapes],
        out_shape=[jax.ShapeDtypeStruct((b, s, w), F32) for w, _ in shapes],
        scratch_shapes=[pltpu.VMEM((t + 2 * SUBLANES, A_CONV_CH), F32)],
        compiler_params=_params(("parallel", "arbitrary")),
        name="inproj",
    )(x, x, x, mod, g_pre, w_in_p, conv_a, ones512, alog_row, dtb_row)


def _bd(x_bf, bm_bf):
    zero = jnp.zeros((CHUNK, LANES), BF16)
    blocks = []
    for h in range(GROUP):
        t = (h * 64) // LANES
        lanes = slice(t * LANES, (t + 1) * LANES)
        piece = x_bf[:, lanes] * bm_bf[h * CHUNK:(h + 1) * CHUNK, lanes]
        blocks.append(jnp.concatenate([piece, zero] if t == 0 else [zero, piece], axis=1))
    return jnp.concatenate(blocks, axis=0)


def _hdot(a, b_bf_bd):
    return _dot(_bf(a), b_bf_bd)


_PAIRS = tuple(slice(p * LANES, (p + 1) * LANES) for p in range(GW // LANES))


def _chunk_rows(c, n_chunks, rev):
    idx = (n_chunks - 1 - c) if rev else c
    return pl.multiple_of(idx * CHUNK, CHUNK)


def _where_mask(m, x, other):
    return jnp.where(m > 0.5, x, other)


def _otimes(a, b, bm):
    return _dot(_bf(a), _bd(_bf(b), bm))


def _delta_local(q, k, v, g_e, b_e, masks, tri, bm, rev):
    gam = _cdot_hl(tri, g_e)
    kb = k * b_e
    kq = _dot_nt(_bf(jnp.concatenate([kb, q], axis=0)), _bd(_bf(k), bm))
    yield
    gj = jnp.sum(gam * masks[M_EYE], axis=0, keepdims=True)
    decay = jnp.exp(_where_mask(masks[M_INCL], gam - gj, -jnp.inf))
    eg = jnp.exp(gam)
    m = kq[0:CHUNK] * decay * masks[M_STRICT]
    aqk = kq[CHUNK:] * decay
    last = 0 if rev else CHUNK - 1
    glast = gam[last:last + 1]
    qd = q * eg
    kd = k * jnp.exp(glast - gam)
    cd = jnp.exp(glast)
    d = m * masks[M_BLK8]
    p1 = _otimes(d, d, bm)
    yield
    x = masks[M_EYE] - d
    both = _otimes(jnp.concatenate([p1, x], axis=0), p1, bm)
    p2 = both[0:CHUNK]
    x = x + both[CHUNK:]
    yield
    x = x + _otimes(x, p2, bm)
    yield
    for s in (8, 16, 32):
        e = m * masks[_OFF_INDEX[s]]
        ex = _otimes(e, x, bm)
        yield
        x = x - _otimes(x, ex, bm)
        yield
    wv = _otimes(x, v * b_e, bm)
    kc = _otimes(x, kb * eg, bm)
    yield
    return wv, kc, aqk, qd, kd, cd


def _delta_step(local, state_ref, idx, o_ref, rows, cols, bm, bm_f32):
    wv, kc, aqk, qd, kd, cd = local
    lhs = _bf(jnp.concatenate([kc, qd], axis=0))
    pairs = [slice(p * LANES, (p + 1) * LANES) for p in range(GW // LANES)]
    states = [state_ref[idx * len(pairs) + p] for p in range(len(pairs))]
    ks_qs = jnp.concatenate([_dot(lhs[:, lanes], _bf(st)) for lanes, st in zip(pairs, states)], axis=1)
    yield
    vnew_bf = _bf(wv - ks_qs[0:CHUNK])
    o_ref[0, rows, cols] = ks_qs[CHUNK:] + _dot(_bf(aqk), _bd(vnew_bf, bm))
    kd_bf = _bf(kd)
    pair_mask = bm_f32[0:LANES, 0:LANES]
    for p, (lanes, st) in enumerate(zip(pairs, states)):
        state_ref[idx * len(pairs) + p] = (st * cd[:, lanes]
                                           + _dot_tn(kd_bf[:, lanes], vnew_bf[:, lanes]) * pair_mask)
    yield


def _interleave(gens):
    out = [None] * len(gens)
    live = list(enumerate(gens))
    while live:
        nxt = []
        for i, g in live:
            try:
                next(g)
                nxt.append((i, g))
            except StopIteration as stop:
                out[i] = stop.value
        live = nxt
    return out


def _mixer_a_kernel(qf_ref, kf_ref, vf_ref, gf_ref, qb_ref, kb_ref, vb_ref, gb_ref,
                    e0_ref, e1_ref, m0_ref, m1_ref, t0_ref, t1_ref, bm_ref, bmf_ref,
                    of_ref, ob_ref, state_ref, ge_ref):
    tb = qf_ref.shape[1]
    n_chunks = tb // CHUNK
    n_groups = A_HEADS // GROUP

    @pl.when(pl.program_id(1) == 0)
    def _():
        state_ref[...] = jnp.zeros_like(state_ref)

    for d, (g_ref, e_ref) in enumerate(((gf_ref, e0_ref), (gb_ref, e1_ref))):
        ge_ref[d, :, 0:A_VW] = _dotc_hl(g_ref[0], e_ref[:, 0:A_VW])
        ge_ref[d, :, A_VW:] = _dot(_bf(g_ref[0]), e_ref[:, A_VW:])
    bm = bm_ref[...]
    bm_f32 = bmf_ref[...]
    dirs = ((qf_ref, kf_ref, vf_ref, of_ref, m0_ref, t0_ref), (qb_ref, kb_ref, vb_ref, ob_ref, m1_ref, t1_ref))

    def body(it, carry):
        units = []
        for j in range(A_CHUNKS_PER_ITER):
            for d, (q_ref, k_ref, v_ref, o_ref, m_ref, t_ref) in enumerate(dirs):
                rows = pl.ds(_chunk_rows(it * A_CHUNKS_PER_ITER + j, n_chunks, d == 1), CHUNK)
                for hg in range(n_groups):
                    cols = slice(hg * GW, (hg + 1) * GW)
                    units.append((j, d, hg, rows, cols))
        local = _interleave([
            _delta_local(dirs[d][0][0, rows, cols], dirs[d][1][0, rows, cols], dirs[d][2][0, rows, cols],
                         ge_ref[d, rows, cols], ge_ref[d, rows, A_VW + hg * GW:A_VW + (hg + 1) * GW],
                         dirs[d][4], dirs[d][5][...], bm, d == 1)
            for (j, d, hg, rows, cols) in units])
        for j in range(A_CHUNKS_PER_ITER):
            _interleave([_delta_step(local[u], state_ref, d * n_groups + hg, dirs[d][3], rows, cols, bm, bm_f32)
                         for u, (ju, d, hg, rows, cols) in enumerate(units) if ju == j])
        return carry

    lax.fori_loop(0, n_chunks // A_CHUNKS_PER_ITER, body, 0)


def _mixer_a(q, k, v, ga, consts):
    b, s, _ = q.shape
    tb = T_MIX
    nb = s // tb
    fwd = lambda bi, i: (bi, i, 0)
    bwd = lambda bi, i: (bi, nb - 1 - i, 0)
    full = lambda arr: pl.BlockSpec(arr.shape, lambda bi, i: (0,) * arr.ndim)
    blk = lambda w, m: pl.BlockSpec((1, tb, w), m)
    cs = [consts["exp_a0"], consts["exp_a1"], consts["masks0"], consts["masks1"], consts["tri0"], consts["tri1"],
          consts["bm"], consts["bm_f32"]]
    return pl.pallas_call(
        _mixer_a_kernel,
        grid=(b, nb),
        in_specs=[blk(A_QK, fwd), blk(A_QK, fwd), blk(A_VW, fwd), blk(LANES, fwd),
                  blk(A_QK, bwd), blk(A_QK, bwd), blk(A_VW, bwd), blk(LANES, bwd)] + [full(a) for a in cs],
        out_specs=[blk(A_VW, fwd), blk(A_VW, bwd)],
        out_shape=[jax.ShapeDtypeStruct((b, s, A_VW), F32)] * 2,
        scratch_shapes=[pltpu.VMEM((2 * (A_HEADS // GROUP) * (GW // LANES), LANES, LANES), F32),
                        pltpu.VMEM((2, tb, 2 * A_VW), F32)],
        compiler_params=_params(("parallel", "arbitrary")),
        name="mixer_delta",
    )(q, k, v, ga, q, k, v, ga, *cs)


def _gla_local(q, k, v, g, masks, glacs, bm, bmk, bmk_f32, rev):
    cs = _cdot_hl(glacs, g)
    k_bf = _bf(k)
    attn = _dot_nt(_bf(q), _tile_rows(k_bf, GROUP) * bmk) * masks[M_EYE]
    yield
    b = cs[0:CHUNK]
    for li, s in enumerate(GLA_LEVELS):
        qe = cs[(1 + 2 * li) * CHUNK:(2 + 2 * li) * CHUNK]
        ke = cs[(2 + 2 * li) * CHUNK:(3 + 2 * li) * CHUNK]
        kl = _tile_rows(_bf(k * jnp.exp(ke)), GROUP) * bmk
        attn = attn + _dot_nt(_bf(q * jnp.exp(qe)), kl) * masks[_OFF_INDEX[s]]
        if li % 2 == 1:
            yield
    last = 0 if rev else CHUNK - 1
    blast = b[last:last + 1]
    qd_bf = _bf(q * jnp.exp(b))
    kd = k * jnp.exp(blast - b)
    cd = jnp.exp(blast)
    v_bf = _bf(v)
    o_intra = _dot(_bf(attn), _bd(v_bf, bm))
    upd = _dot_tn(v_bf, _bf(kd)) * bmk_f32
    yield
    return o_intra, qd_bf, upd, cd


def _gla_step(local, state_ref, d, o_ref, rows):
    o_intra, qd_bf, upd, cd = local
    state_t = state_ref[d]
    o_ref[0, rows, :] = o_intra + _dot_nt(qd_bf, _bf(state_t))
    state_ref[d] = state_t * cd + upd
    yield


def _mixer_b_kernel(pf_ref, pb_ref, wup_ref, bg_ref, m0_ref, m1_ref, c0_ref, c1_ref, bm_ref, bmk_ref, bmkf_ref,
                    of_ref, ob_ref, state_ref, gk_ref):
    tb = pf_ref.shape[1]
    n_chunks = tb // CHUNK

    @pl.when(pl.program_id(1) == 0)
    def _():
        state_ref[...] = jnp.zeros_like(state_ref)

    lr_col = 2 * B_QK + 2 * B_VW
    for d, p_ref in enumerate((pf_ref, pb_ref)):
        lr = p_ref[0, :, lr_col:lr_col + LANES]
        w = wup_ref[d]
        l_hi, l_lo = _split2(lr)
        w_hi, w_lo = _split2(w)
        zz = _dot(l_hi, w_hi) + _dot(l_hi, w_lo) + _dot(l_lo, w_hi) + bg_ref[d]
        gk_ref[d] = _log_sigmoid(zz) * (1.0 / B_TAU)
    bm = bm_ref[...]
    bmk = bmk_ref[...]
    bmk_f32 = bmkf_ref[...]

    dirs = ((pf_ref, of_ref, m0_ref, c0_ref), (pb_ref, ob_ref, m1_ref, c1_ref))

    def body(it, carry):
        units = []
        for j in range(BC_CHUNKS_PER_ITER):
            for d in range(2):
                units.append((j, d, pl.ds(_chunk_rows(it * BC_CHUNKS_PER_ITER + j, n_chunks, d == 1), CHUNK)))
        local = _interleave([
            _gla_local(dirs[d][0][0, rows, 0:B_QK] * (B_DK ** -0.5), dirs[d][0][0, rows, B_QK:2 * B_QK],
                       dirs[d][0][0, rows, 2 * B_QK:2 * B_QK + B_VW], gk_ref[d, rows, :],
                       dirs[d][2], dirs[d][3][...], bm, bmk, bmk_f32, d == 1)
            for (j, d, rows) in units])
        for j in range(BC_CHUNKS_PER_ITER):
            _interleave([_gla_step(local[u], state_ref, d, dirs[d][1], rows)
                         for u, (ju, d, rows) in enumerate(units) if ju == j])
        return carry

    lax.fori_loop(0, n_chunks // BC_CHUNKS_PER_ITER, body, 0)


def _mixer_b(pb, wup_pad, bgla_rows, consts):
    b, s, _ = pb.shape
    tb = T_MIX
    nb = s // tb
    fwd = lambda bi, i: (bi, i, 0)
    bwd = lambda bi, i: (bi, nb - 1 - i, 0)
    full = lambda arr: pl.BlockSpec(arr.shape, lambda bi, i: (0,) * arr.ndim)
    cs = [wup_pad, bgla_rows, consts["masks0"], consts["masks1"], consts["glacs0"], consts["glacs1"],
          consts["bm"], consts["bmk"], consts["bmk_f32"]]
    return pl.pallas_call(
        _mixer_b_kernel,
        grid=(b, nb),
        in_specs=[pl.BlockSpec((1, tb, W_B), fwd), pl.BlockSpec((1, tb, W_B), bwd)] + [full(a) for a in cs],
        out_specs=[pl.BlockSpec((1, tb, B_VW), fwd), pl.BlockSpec((1, tb, B_VW), bwd)],
        out_shape=[jax.ShapeDtypeStruct((b, s, B_VW), F32)] * 2,
        scratch_shapes=[pltpu.VMEM((2, B_VW, B_QK), F32), pltpu.VMEM((2, tb, B_QK), F32)],
        compiler_params=_params(("parallel", "arbitrary")),
        name="mixer_gla",
    )(pb, pb, *cs)


def _cummax_rows(u, rev):
    n = u.shape[0]
    row = lax.broadcasted_iota(jnp.int32, u.shape, 0)
    x = u
    step = 1
    while step < n:
        if step % SUBLANES == 0:
            fill = jnp.full((step,) + u.shape[1:], -jnp.inf, u.dtype)
            sh = jnp.concatenate([x[step:], fill] if rev else [fill, x[:n - step]], axis=0)
        elif rev:
            sh = jnp.where(row < n - step, pltpu.roll(x, n - step, 0), -jnp.inf)
        else:
            sh = jnp.where(row >= step, pltpu.roll(x, step, 0), -jnp.inf)
        x = jnp.maximum(x, sh)
        step *= 2
    return x


def _mlstm_local(q, k, v, ip_e, lf_e, masks, tri, bm, bm_f32, rev):
    fc = _cdot_hl(tri, lf_e)
    q_bf = _bf(q)
    qk = _dot_nt(q_bf, _bd(_bf(k), bm))
    yield
    u = ip_e - fc
    cm = _cummax_rows(u, rev)
    dmax = fc + cm
    uj = jnp.sum(u * masks[M_EYE], axis=0, keepdims=True)
    p_bf = _bf(qk * jnp.exp(_where_mask(masks[M_INCL], uj - cm, -jnp.inf)))
    last = 0 if rev else CHUNK - 1
    flast = fc[last:last + 1]
    gmax = flast + cm[last:last + 1]
    v_bf = _bf(v)
    pv = _dot(p_bf, _bd(v_bf, bm))
    psum = _dot(p_bf, bm)
    kg = k * jnp.exp(flast + u - gmax)
    kg_bf = _bf(kg)
    pair_mask = bm_f32[0:LANES, 0:LANES]
    w = [_dot_tn(kg_bf[:, lanes], v_bf[:, lanes]) * pair_mask for lanes in _PAIRS]
    ksum = jnp.sum(kg, axis=0, keepdims=True)
    yield
    return q, q_bf, fc, dmax, flast, gmax, pv, psum, w, ksum


def _mlstm_step(local, c_ref, n_ref, mx_ref, d, o_ref, rows, bm):
    q, q_bf, fc, dmax, flast, gmax, pv, psum, w, ksum = local
    nrow, mrow = n_ref[d], mx_ref[d]
    cmats = [c_ref[d * len(_PAIRS) + p] for p in range(len(_PAIRS))]
    qc = jnp.concatenate([_dot(q_bf[:, lanes], _bf(cm)) for lanes, cm in zip(_PAIRS, cmats)], axis=1)
    qn = _dot(_bf(q * nrow), bm)
    yield
    m_inter = fc + mrow
    m_i = jnp.maximum(dmax, m_inter)
    s_inter = jnp.exp(m_inter - m_i)
    s_intra = jnp.exp(dmax - m_i)
    num = s_inter * qc + s_intra * pv
    den = s_inter * qn + s_intra * psum
    o_ref[0, rows, :] = num / jnp.maximum(jnp.abs(den), jnp.exp(-m_i))
    m_new = jnp.maximum(flast + mrow, gmax)
    a = jnp.exp(flast + mrow - m_new)
    sc = jnp.exp(gmax - m_new)
    for p, (lanes, cm) in enumerate(zip(_PAIRS, cmats)):
        c_ref[d * len(_PAIRS) + p] = a[:, lanes] * cm + sc[:, lanes] * w[p]
    n_ref[d] = a * nrow + sc * ksum
    mx_ref[d] = m_new
    yield


def _mixer_c_kernel(pf_ref, pb_ref, e0_ref, e1_ref, big_ref, bfg_ref, m0_ref, m1_ref, t0_ref, t1_ref, bm_ref,
                    bmf_ref, of_ref, ob_ref, c_ref, n_ref, mx_ref, ge_ref):
    tb = pf_ref.shape[1]
    n_chunks = tb // CHUNK

    @pl.when(pl.program_id(1) == 0)
    def _():
        c_ref[...] = jnp.zeros_like(c_ref)
        n_ref[...] = jnp.zeros_like(n_ref)
        mx_ref[...] = jnp.zeros_like(mx_ref)

    gate_col = 2 * C_QK + 2 * C_VW
    for d, (p_ref, e_ref) in enumerate(((pf_ref, e0_ref), (pb_ref, e1_ref))):
        gt = p_ref[0, :, gate_col:gate_col + LANES]
        col = lax.broadcasted_iota(jnp.int32, gt.shape, 1)
        gates = jnp.where(col < 2 * C_HEADS, gt + big_ref[...], _log_sigmoid(gt + bfg_ref[...]))
        ge_ref[d] = _dotc_hl(gates, e_ref[...])
    bm = bm_ref[...]
    bm_f32 = bmf_ref[...]
    dirs = ((pf_ref, of_ref, m0_ref, t0_ref), (pb_ref, ob_ref, m1_ref, t1_ref))

    def body(it, carry):
        units = []
        for j in range(BC_CHUNKS_PER_ITER):
            for d in range(2):
                units.append((j, d, pl.ds(_chunk_rows(it * BC_CHUNKS_PER_ITER + j, n_chunks, d == 1), CHUNK)))
        local = _interleave([
            _mlstm_local(dirs[d][0][0, rows, 0:C_QK], dirs[d][0][0, rows, C_QK:2 * C_QK] * (C_DK ** -0.5),
                         dirs[d][0][0, rows, 2 * C_QK:2 * C_QK + C_VW],
                         ge_ref[d, rows, 0:C_VW], ge_ref[d, rows, C_VW:2 * C_VW],
                         dirs[d][2], dirs[d][3][...], bm, bm_f32, d == 1)
            for (j, d, rows) in units])
        for j in range(BC_CHUNKS_PER_ITER):
            _interleave([_mlstm_step(local[u], c_ref, n_ref, mx_ref, d, dirs[d][1], rows, bm)
                         for u, (ju, d, rows) in enumerate(units) if ju == j])
        return carry

    lax.fori_loop(0, n_chunks // BC_CHUNKS_PER_ITER, body, 0)


def _mixer_c(pc, big_row, bfg_row, consts):
    b, s, _ = pc.shape
    tb = T_MIX
    nb = s // tb
    fwd = lambda bi, i: (bi, i, 0)
    bwd = lambda bi, i: (bi, nb - 1 - i, 0)
    full = lambda arr: pl.BlockSpec(arr.shape, lambda bi, i: (0,) * arr.ndim)
    cs = [consts["exp_c0"], consts["exp_c1"], big_row, bfg_row, consts["masks0"], consts["masks1"],
          consts["tri0"], consts["tri1"], consts["bm"], consts["bm_f32"]]
    return pl.pallas_call(
        _mixer_c_kernel,
        grid=(b, nb),
        in_specs=[pl.BlockSpec((1, tb, W_C), fwd), pl.BlockSpec((1, tb, W_C), bwd)] + [full(a) for a in cs],
        out_specs=[pl.BlockSpec((1, tb, C_VW), fwd), pl.BlockSpec((1, tb, C_VW), bwd)],
        out_shape=[jax.ShapeDtypeStruct((b, s, C_VW), F32)] * 2,
        scratch_shapes=[pltpu.VMEM((2 * len(_PAIRS), LANES, LANES), F32), pltpu.VMEM((2, 1, GW), F32),
                        pltpu.VMEM((2, 1, GW), F32),
                        pltpu.VMEM((2, tb, 2 * C_VW), F32)],
        compiler_params=_params(("parallel", "arbitrary")),
        name="mixer_mlstm",
    )(pc, pc, *cs)


def _outproj_kernel(x_ref, mod_ref, af_ref, ab_ref, z_ref, bf_ref, bb_ref, bg_ref, cf_ref, cb_ref, co_ref,
                    na_ref, nb_ref, nc_ref, w_ref, gpost_ref, ones_ref, o_ref):
    ones_bd = ones_ref[...]
    oa = af_ref[0] + ab_ref[0]
    ms = _seg_sum64(oa * oa, ones_bd) * (1.0 / A_DV)
    oa = oa * lax.rsqrt(ms + EPS) * na_ref[...] * _silu(z_ref[0])
    ones_q = ones_bd[0:B_VW, 0:B_VW]
    ob = bf_ref[0] + bb_ref[0]
    ms = _seg_sum64(ob * ob, ones_q) * (1.0 / B_DV)
    ob = ob * lax.rsqrt(ms + EPS) * nb_ref[...] * _silu(bg_ref[0])
    hc = _sigmoid(co_ref[0]) * (cf_ref[0] + cb_ref[0])
    mu = _seg_sum64(hc, ones_q) * (1.0 / C_DV)
    dv = hc - mu
    var = _seg_sum64(dv * dv, ones_q) * (1.0 / C_DV)
    oc = dv * lax.rsqrt(var + EPS) * nc_ref[...]
    y = (_dot(_bf(oa), w_ref[0:A_VW, :]) + _dot(_bf(ob), w_ref[A_VW:A_VW + B_VW, :])
         + _dot(_bf(oc), w_ref[A_VW + B_VW:, :]))
    gt1 = mod_ref[0][2:3]
    o_ref[0] = x_ref[0] + gt1 * _rms(y, gpost_ref[...])


def _outproj(x, mod, oaf, oab, z, obf, obb, pb, ocf, ocb, pc, na_row, nb_row, nc_row, w_out_bf, g_post, ones512):
    b, s, d = x.shape
    t = T_OUT
    tok = lambda w, j=0: pl.BlockSpec((1, t, w), lambda bi, i: (bi, i, j))
    full = lambda arr: pl.BlockSpec(arr.shape, lambda bi, i: (0,) * arr.ndim)
    bg_blk = (2 * B_QK + B_VW) // B_VW
    co_blk = (2 * C_QK + C_VW) // C_VW
    return pl.pallas_call(
        _outproj_kernel,
        grid=(b, s // t),
        in_specs=[tok(d), pl.BlockSpec((1, 6, d), lambda bi, i: (bi, 0, 0)),
                  tok(A_VW), tok(A_VW), tok(A_VW), tok(B_VW), tok(B_VW), tok(B_VW, bg_blk),
                  tok(C_VW), tok(C_VW), tok(C_VW, co_blk),
                  full(na_row), full(nb_row), full(nc_row), full(w_out_bf), full(g_post), full(ones512)],
        out_specs=tok(d),
        out_shape=jax.ShapeDtypeStruct((b, s, d), F32),
        compiler_params=_params(("parallel", "arbitrary")),
        name="outproj",
    )(x, mod, oaf, oab, z, obf, obb, pb, ocf, ocb, pc, na_row, nb_row, nc_row, w_out_bf, g_post, ones512)


def _ffn_kernel(xp_ref, x_ref, xn_ref, mod_ref, gpre_ref, wu_ref, cw_ref, cb_ref, wd_ref, gpost_ref,
                o_ref, h_ref, u_ref, acc_ref):
    i = pl.program_id(1)
    nt = pl.num_programs(1)
    f = pl.program_id(2)
    nf = pl.num_programs(2)
    t = x_ref.shape[1]
    n_slabs = wu_ref.shape[1] // (2 * LANES)
    mod = mod_ref[0]

    @pl.when(f == 0)
    def _():
        xe = jnp.concatenate([xp_ref[0], x_ref[0], xn_ref[0]], axis=0)
        h_ref[...] = _bf(_rms(xe, gpre_ref[...]) * (1.0 + mod[4:5]) + mod[3:4])
        acc_ref[...] = jnp.zeros_like(acc_ref)

    h = h_ref[...]
    lo_ok = (i > 0).astype(F32)
    hi_ok = (i < nt - 1).astype(F32)

    def up(j):
        cols = slice(j * 2 * LANES, (j + 1) * 2 * LANES)
        u = _dot(h, wu_ref[:, cols])
        u_ref[0:SUBLANES, cols] = u[0:SUBLANES] * lo_ok
        u_ref[SUBLANES:t + SUBLANES, cols] = u[SUBLANES:t + SUBLANES]
        u_ref[t + SUBLANES:t + 2 * SUBLANES, cols] = u[t + SUBLANES:] * hi_ok

    def gated(j):
        cols = slice(j * 2 * LANES, (j + 1) * 2 * LANES)
        cw = cw_ref[:, cols]
        cv = (cw[0:1] * u_ref[pl.ds(SUBLANES - 1, t), cols] + cw[1:2] * u_ref[pl.ds(SUBLANES, t), cols]
              + cw[2:3] * u_ref[pl.ds(SUBLANES + 1, t), cols] + cb_ref[:, cols])
        gate = cv[:, 0:LANES]
        return _bf(0.5 * gate * (1.0 + lax.erf(gate * (2.0 ** -0.5))) * cv[:, LANES:])

    def down(j0, j1):
        act = jnp.concatenate([gated(j) for j in range(j0, j1)], axis=1)
        acc_ref[...] += _dot(act, wd_ref[j0 * LANES:j1 * LANES, :])

    for j in range(min(FFN_LAG, n_slabs)):
        up(j)
    for j0 in range(0, n_slabs, 2):
        for j in range(j0 + FFN_LAG, min(j0 + FFN_LAG + 2, n_slabs)):
            up(j)
        down(j0, min(j0 + 2, n_slabs))

    @pl.when(f == nf - 1)
    def _():
        o_ref[0] = x_ref[0] + mod[5:6] * _rms(acc_ref[...], gpost_ref[...])


def _pair_gate_value(a):
    lead = a.shape[:-1]
    g = a[..., :D_FF].reshape(lead + (D_FF // LANES, 1, LANES))
    v = a[..., D_FF:].reshape(lead + (D_FF // LANES, 1, LANES))
    return jnp.concatenate([g, v], axis=-2).reshape(lead + (2 * D_FF,))


def _ffn(x, mod, g_pre, w_up_pair_bf, conv_w_pair, conv_b_pair, w_down_bf, g_post):
    b, s, d = x.shape
    t = T_FFN
    ft = F_TILE
    nf = D_FF // ft
    tb = t // SUBLANES
    nblk8 = s // SUBLANES
    full = lambda shape: pl.BlockSpec(shape, lambda bi, i, f: (0,) * len(shape))
    return pl.pallas_call(
        _ffn_kernel,
        grid=(b, s // t, nf),
        in_specs=[pl.BlockSpec((1, SUBLANES, d), lambda bi, i, f: (bi, jnp.maximum(i * tb - 1, 0), 0)),
                  pl.BlockSpec((1, t, d), lambda bi, i, f: (bi, i, 0)),
                  pl.BlockSpec((1, SUBLANES, d), lambda bi, i, f: (bi, jnp.minimum((i + 1) * tb, nblk8 - 1), 0)),
                  pl.BlockSpec((1, 6, d), lambda bi, i, f: (bi, 0, 0)),
                  full((1, d)),
                  pl.BlockSpec((d, 2 * ft), lambda bi, i, f: (0, f)),
                  pl.BlockSpec((3, 2 * ft), lambda bi, i, f: (0, f)),
                  pl.BlockSpec((1, 2 * ft), lambda bi, i, f: (0, f)),
                  pl.BlockSpec((ft, d), lambda bi, i, f: (f, 0)),
                  full((1, d))],
        out_specs=pl.BlockSpec((1, t, d), lambda bi, i, f: (bi, i, 0)),
        out_shape=jax.ShapeDtypeStruct((b, s, d), F32),
        scratch_shapes=[pltpu.VMEM((t + 2 * SUBLANES, d), BF16),
                        pltpu.VMEM((t + 2 * SUBLANES, 2 * ft), F32),
                        pltpu.VMEM((t, d), F32)],
        compiler_params=_params(("parallel", "arbitrary", "arbitrary")),
        name="ffn",
    )(x, x, x, mod, g_pre, w_up_pair_bf, conv_w_pair, conv_b_pair, w_down_bf, g_post)


def _pad_cols(w, sections):
    parts, start = [], 0
    for width, padded in sections:
        parts.append(w[:, start:start + width])
        if padded > width:
            parts.append(jnp.zeros((w.shape[0], padded - width), w.dtype))
        start += width
    return jnp.concatenate(parts, axis=1)


def _row(v, width=None):
    v = v.reshape(1, -1).astype(F32)
    if width is not None and v.shape[1] < width:
        v = jnp.pad(v, ((0, 0), (0, width - v.shape[1])))
    return v


def kernel(x_prompt, x_sample, c_prompt, c_sample, w_ada, b_ada, g_pre_mix, w_in, conv_a, a_log, dt_bias, norm_a,
           w_gla_up, b_gla, norm_b, bias_ig, bias_fg, norm_c, w_out, g_post_mix, g_pre_ffn, w_up, conv_ffn,
           b_conv_ffn, w_down, g_post_ffn):
    consts = _constants()
    depth = w_ada.shape[0]
    d = D_MODEL
    n_prompt = x_prompt.shape[0]
    c_all = jnp.concatenate([c_prompt, c_sample], axis=0)
    xs = [x_prompt, x_sample]
    a_in = A_CONV_CH + A_VW + 4 * A_HEADS
    b_in = 2 * B_QK + 2 * B_VW + 2 * B_RANK
    c_in = 2 * C_QK + 2 * C_VW + 4 * C_HEADS
    for l in range(depth):
        mod_all = _mod(c_all, w_ada[l], b_ada[l])
        mods = [mod_all[:n_prompt].reshape(-1, 6, d), mod_all[n_prompt:].reshape(-1, 6, d)]
        w_in_p = _pad_cols(w_in[l], [(a_in, A_CONV_CH + W_REST_A), (b_in, W_B), (c_in, W_C)]).astype(BF16)
        w_out_bf = w_out[l].astype(BF16)
        w_up_bf = _pair_gate_value(w_up[l]).astype(BF16)
        conv_ffn_pair = _pair_gate_value(conv_ffn[l])
        b_ffn_pair = _pair_gate_value(_row(b_conv_ffn[l]))
        w_down_bf = w_down[l].astype(BF16)
        alog_row = _row(a_log[l], LANES)
        dtb_row = _row(dt_bias[l], LANES)
        wup_pad = jnp.zeros((2, LANES, B_QK), F32)
        for dd in range(2):
            wup_pad = wup_pad.at[dd, dd * B_RANK:(dd + 1) * B_RANK, :].set(w_gla_up[l, dd])
        bgla_rows = b_gla[l].reshape(2, 1, B_QK).astype(F32)
        big_row = _row(bias_ig[l], LANES)
        bfg_row = jnp.pad(bias_fg[l].reshape(1, -1).astype(F32), ((0, 0), (2 * C_HEADS, LANES - 4 * C_HEADS)))
        na_row = _row(jnp.tile(norm_a[l], A_HEADS))
        nb_row = _row(jnp.tile(norm_b[l], B_HEADS))
        nc_row = _row(norm_c[l])
        for gi in range(2):
            x, mod = xs[gi], mods[gi]
            q, k, v, z, ga, pb, pc = _inproj(x, mod, _row(g_pre_mix[l]), w_in_p, conv_a[l], consts["ones512"],
                                             alog_row, dtb_row)
            oaf, oab = _mixer_a(q, k, v, ga, consts)
            obf, obb = _mixer_b(pb, wup_pad, bgla_rows, consts)
            ocf, ocb = _mixer_c(pc, big_row, bfg_row, consts)
            x = _outproj(x, mod, oaf, oab, z, obf, obb, pb, ocf, ocb, pc, na_row, nb_row, nc_row, w_out_bf,
                         _row(g_post_mix[l]), consts["ones512"])
            x = _ffn(x, mod, _row(g_pre_ffn[l]), w_up_bf, conv_ffn_pair, b_ffn_pair, w_down_bf,
                     _row(g_post_ffn[l]))
            xs[gi] = x
    return (xs[0], xs[1])
```

```python
import functools
import math

import numpy as np
import jax
import jax.numpy as jnp
from jax import lax
from jax.experimental import pallas as pl
from jax.experimental.pallas import tpu as pltpu

F32 = jnp.float32
BF16 = jnp.bfloat16

D_MODEL = 1024
A_HEADS, A_DK, A_DV = 8, 64, 64
B_HEADS, B_DK, B_DV, B_RANK, B_TAU = 4, 32, 64, 16, 16.0
C_HEADS, C_DK, C_DV = 4, 64, 64
D_FF = 2688
EPS = 1e-6

A_QK = A_HEADS * A_DK
A_VW = A_HEADS * A_DV
A_CONV_CH = 2 * A_QK + A_VW
B_QK = B_HEADS * B_DK
B_VW = B_HEADS * B_DV
C_QK = C_HEADS * C_DK
C_VW = C_HEADS * C_DV

LANES = 128
SUBLANES = 8
CHUNK = 64
GROUP = 4
GW = GROUP * 64

W_REST_A = A_VW + LANES
W_B = 2 * B_QK + 2 * B_VW + LANES
W_C = 2 * C_QK + 2 * C_VW + LANES
W_IN_PAD = A_CONV_CH + W_REST_A + W_B + W_C

T_PROJ = 512
T_MIX = 512
T_OUT = 512
T_FFN = 1024
F_TILE = 896
FFN_LAG = 4
VMEM_LIMIT = 56 * 1024 * 1024
MIX_OUT_DTYPE = BF16
A_CHUNKS_PER_ITER = 4
BC_CHUNKS_PER_ITER = 4


def _dot(a, b):
    return jnp.dot(a, b, preferred_element_type=F32)


def _dot_nt(a, b):
    return lax.dot_general(a, b, (((1,), (1,)), ((), ())), preferred_element_type=F32)


def _dot_tn(a, b):
    return lax.dot_general(a, b, (((0,), (0,)), ((), ())), preferred_element_type=F32)


def _bf(x):
    return x.astype(BF16)


def _split2(x):
    hi = _bf(x)
    lo = _bf(x - hi.astype(F32))
    return hi, lo


def _cdot_hl(c, x):
    hi, lo = _split2(x)
    return _dot(c, hi) + _dot(c, lo)


def _dotc_hl(x, c):
    hi, lo = _split2(x)
    return _dot(hi, c) + _dot(lo, c)


def _tile_rows(x, n):
    return jnp.concatenate([x] * n, axis=0)


def _silu(x):
    return x * (1.0 / (1.0 + jnp.exp(-x)))


def _sigmoid(x):
    return 1.0 / (1.0 + jnp.exp(-x))


def _softplus(x):
    return jnp.maximum(x, 0.0) + jnp.log(1.0 + jnp.exp(-jnp.abs(x)))


def _log_sigmoid(x):
    return -_softplus(-x)


def _rms(x, g_row):
    return x * lax.rsqrt(jnp.mean(x * x, axis=-1, keepdims=True) + EPS) * g_row


def _params(sem):
    return pltpu.CompilerParams(dimension_semantics=sem, vmem_limit_bytes=VMEM_LIMIT)


def _dir_masks(rev):
    i = np.arange(CHUNK)[:, None]
    j = np.arange(CHUNK)[None, :]
    if rev:
        i, j = CHUNK - 1 - i, CHUNK - 1 - j
    incl = j <= i
    strict = j < i
    rows = [incl, strict, np.eye(CHUNK, dtype=bool), (i // 8 == j // 8) & strict]
    for s in (8, 16, 32, 1, 2, 4):
        rows.append((i // (2 * s) == j // (2 * s)) & (i // s == j // s + 1))
    return np.stack([np.tile(r, (1, GROUP)) for r in rows]).astype(np.float32), incl.astype(np.float32)


M_INCL, M_STRICT, M_EYE, M_BLK8, M_OFF8, M_OFF16, M_OFF32, M_OFF1, M_OFF2, M_OFF4 = range(10)
_OFF_INDEX = {8: M_OFF8, 16: M_OFF16, 32: M_OFF32, 1: M_OFF1, 2: M_OFF2, 4: M_OFF4}


def _gla_cumsum_mats(rev):
    i = np.arange(CHUNK)[:, None]
    t = np.arange(CHUNK)[None, :]
    mats = []
    for s in (32, 16, 8, 4, 2, 1):
        blk = (i // (2 * s)) * 2 * s
        upper = (i % (2 * s)) >= s
        csq = upper & (t >= blk + s) & (t <= i)
        csk = (~upper) & (t >= i + 1) & (t <= blk + s - 1)
        if rev:
            csq, csk = csq[::-1, ::-1], csk[::-1, ::-1]
        mats += [csq, csk]
    return mats


GLA_LEVELS = (32, 16, 8, 4, 2, 1)


def _constants():
    c = {}
    for d, rev in enumerate((False, True)):
        masks, tri = _dir_masks(rev)
        c[f"masks{d}"] = jnp.asarray(masks)
        c[f"tri{d}"] = jnp.asarray(tri, BF16)
        gl = np.concatenate([tri] + [m.astype(np.float32) for m in _gla_cumsum_mats(rev)], axis=0)
        c[f"glacs{d}"] = jnp.asarray(gl, BF16)
    r = np.arange(GW)
    c["bm"] = jnp.asarray((r[:, None] // 64 == r[None, :] // 64), BF16)
    c["bm_f32"] = jnp.asarray((r[:, None] // 64 == r[None, :] // 64), F32)
    c["bmk"] = jnp.asarray((r[:, None] // 64 == np.arange(B_QK)[None, :] // B_DK), BF16)
    c["bmk_f32"] = jnp.asarray((r[:, None] // 64 == np.arange(B_QK)[None, :] // B_DK), F32)
    r5 = np.arange(512)
    c["ones512"] = jnp.asarray((r5[:, None] // 64 == r5[None, :] // 64), BF16)
    for d in range(2):
        ex = np.zeros((LANES, 2 * A_VW), np.float32)
        for q in range(2):
            for h in range(A_HEADS):
                ex[16 * q + 8 * d + h, q * A_VW + h * 64: q * A_VW + (h + 1) * 64] = 1.0
        c[f"exp_a{d}"] = jnp.asarray(ex, BF16)
    for d in range(2):
        ex = np.zeros((LANES, 2 * C_VW), np.float32)
        for q in range(2):
            for h in range(C_HEADS):
                ex[8 * q + 4 * d + h, q * C_VW + h * 64: q * C_VW + (h + 1) * 64] = 1.0
        c[f"exp_c{d}"] = jnp.asarray(ex, BF16)
    return c


def _mod_kernel(c_ref, w_ref, b_ref, o_ref):
    c = c_ref[...]
    s = _silu(c)
    s_hi, s_lo = _split2(s)
    w = w_ref[...]
    w_hi, w_lo = _split2(w)
    o_ref[...] = _dot(s_hi, w_hi) + _dot(s_hi, w_lo) + _dot(s_lo, w_hi) + b_ref[...]


def _mod(c_all, w_ada_l, b_ada_l):
    n, d = c_all.shape
    n6 = w_ada_l.shape[1]
    tn = 1536
    return pl.pallas_call(
        _mod_kernel,
        grid=(n6 // tn,),
        in_specs=[pl.BlockSpec((n, d), lambda j: (0, 0)),
                  pl.BlockSpec((d, tn), lambda j: (0, j)),
                  pl.BlockSpec((1, tn), lambda j: (0, j))],
        out_specs=pl.BlockSpec((n, tn), lambda j: (0, j)),
        out_shape=jax.ShapeDtypeStruct((n, n6), F32),
        compiler_params=_params(("arbitrary",)),
        name="adaln_mod",
    )(c_all, w_ada_l, b_ada_l.reshape(1, n6))


def _seg_sum64(x, ones_bd):
    return _dot(_bf(x), ones_bd)


def _inproj_kernel(xp_ref, x_ref, xn_ref, mod_ref, g_ref, w_ref, conv_ref, ones_ref, alog_ref, dtb_ref,
                   q_ref, k_ref, v_ref, z_ref, ga_ref, pb_ref, pc_ref, scr_ref):
    i = pl.program_id(1)
    nt = pl.num_programs(1)
    t = x_ref.shape[1]
    mod = mod_ref[0]
    sh1, sc1 = mod[0:1], mod[1:2]
    xe = jnp.concatenate([xp_ref[0], x_ref[0], xn_ref[0]], axis=0)
    h = _bf(_rms(xe, g_ref[...]) * (1.0 + sc1) + sh1)
    lo_ok = (i > 0).astype(F32)
    hi_ok = (i < nt - 1).astype(F32)
    for part in range(A_CONV_CH // A_QK):
        cols = slice(part * A_QK, (part + 1) * A_QK)
        p = _dot(h, w_ref[:, cols])
        scr_ref[0:SUBLANES, cols] = p[0:SUBLANES] * lo_ok
        scr_ref[SUBLANES:t + SUBLANES, cols] = p[SUBLANES:t + SUBLANES]
        scr_ref[t + SUBLANES:t + 2 * SUBLANES, cols] = p[t + SUBLANES:] * hi_ok
    hm = h[SUBLANES:t + SUBLANES]
    rest = _dot(hm, w_ref[:, A_CONV_CH:])
    ones_bd = ones_ref[...]

    def conv_silu(cols):
        cw = conv_ref[:, cols]
        return _silu(cw[0:1] * scr_ref[pl.ds(SUBLANES - 1, t), cols] + cw[1:2] * scr_ref[pl.ds(SUBLANES, t), cols]
                     + cw[2:3] * scr_ref[pl.ds(SUBLANES + 1, t), cols])

    q = conv_silu(slice(0, A_QK))
    q_ref[0] = q * lax.rsqrt(_seg_sum64(q * q, ones_bd) + EPS) * (A_DK ** -0.5)
    k = conv_silu(slice(A_QK, 2 * A_QK))
    k_ref[0] = k * lax.rsqrt(_seg_sum64(k * k, ones_bd) + EPS)
    v_ref[0] = conv_silu(slice(2 * A_QK, A_CONV_CH))
    z_ref[0] = rest[:, 0:A_VW].astype(z_ref.dtype)
    gt = rest[:, A_VW:W_REST_A]
    col = lax.broadcasted_iota(jnp.int32, gt.shape, 1)
    gdec = -jnp.exp(alog_ref[...]) * _softplus(gt + dtb_ref[...])
    ga_ref[0] = jnp.where(col < 2 * A_HEADS, gdec, _sigmoid(gt))
    pb_ref[0] = rest[:, W_REST_A:W_REST_A + W_B]
    pc_ref[0] = rest[:, W_REST_A + W_B:]


def _inproj(x, mod, g_pre, w_in_p, conv_a, ones512, alog_row, dtb_row):
    b, s, d = x.shape
    t = T_PROJ
    nt = s // t
    tb = t // SUBLANES
    nblk8 = s // SUBLANES
    shapes = [(A_QK, F32), (A_QK, F32), (A_VW, F32), (A_VW, MIX_OUT_DTYPE), (LANES, F32), (W_B, F32), (W_C, F32)]
    full = lambda shape: pl.BlockSpec(shape, lambda bi, i: (0,) * len(shape))
    return pl.pallas_call(
        _inproj_kernel,
        grid=(b, nt),
        in_specs=[pl.BlockSpec((1, SUBLANES, d), lambda bi, i: (bi, jnp.maximum(i * tb - 1, 0), 0)),
                  pl.BlockSpec((1, t, d), lambda bi, i: (bi, i, 0)),
                  pl.BlockSpec((1, SUBLANES, d), lambda bi, i: (bi, jnp.minimum((i + 1) * tb, nblk8 - 1), 0)),
                  pl.BlockSpec((1, 6, d), lambda bi, i: (bi, 0, 0)),
                  full((1, d)), full((d, W_IN_PAD)), full((3, A_CONV_CH)), full((512, 512)),
                  full((1, LANES)), full((1, LANES))],
        out_specs=[pl.BlockSpec((1, t, w), lambda bi, i: (bi, i, 0)) for w, _ in shapes],
        out_shape=[jax.ShapeDtypeStruct((b, s, w), dt) for w, dt in shapes],
        scratch_shapes=[pltpu.VMEM((t + 2 * SUBLANES, A_CONV_CH), F32)],
        compiler_params=_params(("parallel", "arbitrary")),
        name="inproj",
    )(x, x, x, mod, g_pre, w_in_p, conv_a, ones512, alog_row, dtb_row)


def _bd(x_bf, bm_bf):
    zero = jnp.zeros((CHUNK, LANES), BF16)
    blocks = []
    for h in range(GROUP):
        t = (h * 64) // LANES
        lanes = slice(t * LANES, (t + 1) * LANES)
        piece = x_bf[:, lanes] * bm_bf[h * CHUNK:(h + 1) * CHUNK, lanes]
        blocks.append(jnp.concatenate([piece, zero] if t == 0 else [zero, piece], axis=1))
    return jnp.concatenate(blocks, axis=0)


def _hdot(a, b_bf_bd):
    return _dot(_bf(a), b_bf_bd)


_PAIRS = tuple(slice(p * LANES, (p + 1) * LANES) for p in range(GW // LANES))


def _chunk_rows(c, n_chunks, rev):
    idx = (n_chunks - 1 - c) if rev else c
    return pl.multiple_of(idx * CHUNK, CHUNK)


def _where_mask(m, x, other):
    return jnp.where(m > 0.5, x, other)


def _otimes(a, b, bm):
    return _dot(_bf(a), _bd(_bf(b), bm))


def _delta_local(q, k, v, g_e, b_e, masks, tri, bm, rev):
    gam = _cdot_hl(tri, g_e)
    kb = k * b_e
    kq = _dot_nt(_bf(jnp.concatenate([kb, q], axis=0)), _bd(_bf(k), bm))
    yield
    gj = jnp.sum(gam * masks[M_EYE], axis=0, keepdims=True)
    decay = jnp.exp(_where_mask(masks[M_INCL], gam - gj, -jnp.inf))
    eg = jnp.exp(gam)
    m = kq[0:CHUNK] * decay * masks[M_STRICT]
    aqk = kq[CHUNK:] * decay
    last = 0 if rev else CHUNK - 1
    glast = gam[last:last + 1]
    qd = q * eg
    kd = k * jnp.exp(glast - gam)
    cd = jnp.exp(glast)
    d = m * masks[M_BLK8]
    p1 = _otimes(d, d, bm)
    yield
    x = masks[M_EYE] - d
    both = _otimes(jnp.concatenate([p1, x], axis=0), p1, bm)
    p2 = both[0:CHUNK]
    x = x + both[CHUNK:]
    yield
    x = x + _otimes(x, p2, bm)
    yield
    for s in (8, 16, 32):
        e = m * masks[_OFF_INDEX[s]]
        ex = _otimes(e, x, bm)
        yield
        x = x - _otimes(x, ex, bm)
        yield
    wv = _otimes(x, v * b_e, bm)
    kc = _otimes(x, kb * eg, bm)
    yield
    return wv, kc, aqk, qd, kd, cd


def _delta_step(local, state_ref, idx, o_ref, rows, cols, bm, bm_f32):
    wv, kc, aqk, qd, kd, cd = local
    lhs = _bf(jnp.concatenate([kc, qd], axis=0))
    pairs = [slice(p * LANES, (p + 1) * LANES) for p in range(GW // LANES)]
    states = [state_ref[idx * len(pairs) + p] for p in range(len(pairs))]
    ks_qs = jnp.concatenate([_dot(lhs[:, lanes], _bf(st)) for lanes, st in zip(pairs, states)], axis=1)
    yield
    vnew_bf = _bf(wv - ks_qs[0:CHUNK])
    o_ref[0, rows, cols] = (ks_qs[CHUNK:] + _dot(_bf(aqk), _bd(vnew_bf, bm))).astype(o_ref.dtype)
    kd_bf = _bf(kd)
    pair_mask = bm_f32[0:LANES, 0:LANES]
    for p, (lanes, st) in enumerate(zip(pairs, states)):
        state_ref[idx * len(pairs) + p] = (st * cd[:, lanes]
                                           + _dot_tn(kd_bf[:, lanes], vnew_bf[:, lanes]) * pair_mask)
    yield


def _interleave(gens):
    out = [None] * len(gens)
    live = list(enumerate(gens))
    while live:
        nxt = []
        for i, g in live:
            try:
                next(g)
                nxt.append((i, g))
            except StopIteration as stop:
                out[i] = stop.value
        live = nxt
    return out


def _mixer_a_kernel(qf_ref, kf_ref, vf_ref, gf_ref, qb_ref, kb_ref, vb_ref, gb_ref,
                    e0_ref, e1_ref, m0_ref, m1_ref, t0_ref, t1_ref, bm_ref, bmf_ref,
                    of_ref, ob_ref, state_ref, ge_ref):
    tb = qf_ref.shape[1]
    n_chunks = tb // CHUNK
    n_groups = A_HEADS // GROUP

    @pl.when(pl.program_id(1) == 0)
    def _():
        state_ref[...] = jnp.zeros_like(state_ref)

    for d, (g_ref, e_ref) in enumerate(((gf_ref, e0_ref), (gb_ref, e1_ref))):
        ge_ref[d, :, 0:A_VW] = _dotc_hl(g_ref[0], e_ref[:, 0:A_VW])
        ge_ref[d, :, A_VW:] = _dot(_bf(g_ref[0]), e_ref[:, A_VW:])
    bm = bm_ref[...]
    bm_f32 = bmf_ref[...]
    dirs = ((qf_ref, kf_ref, vf_ref, of_ref, m0_ref, t0_ref), (qb_ref, kb_ref, vb_ref, ob_ref, m1_ref, t1_ref))

    def body(it, carry):
        units = []
        for j in range(A_CHUNKS_PER_ITER):
            for d, (q_ref, k_ref, v_ref, o_ref, m_ref, t_ref) in enumerate(dirs):
                rows = pl.ds(_chunk_rows(it * A_CHUNKS_PER_ITER + j, n_chunks, d == 1), CHUNK)
                for hg in range(n_groups):
                    cols = slice(hg * GW, (hg + 1) * GW)
                    units.append((j, d, hg, rows, cols))
        local = _interleave([
            _delta_local(dirs[d][0][0, rows, cols], dirs[d][1][0, rows, cols], dirs[d][2][0, rows, cols],
                         ge_ref[d, rows, cols], ge_ref[d, rows, A_VW + hg * GW:A_VW + (hg + 1) * GW],
                         dirs[d][4], dirs[d][5][...], bm, d == 1)
            for (j, d, hg, rows, cols) in units])
        for j in range(A_CHUNKS_PER_ITER):
            _interleave([_delta_step(local[u], state_ref, d * n_groups + hg, dirs[d][3], rows, cols, bm, bm_f32)
                         for u, (ju, d, hg, rows, cols) in enumerate(units) if ju == j])
        return carry

    lax.fori_loop(0, n_chunks // A_CHUNKS_PER_ITER, body, 0)


def _mixer_a(q, k, v, ga, consts):
    b, s, _ = q.shape
    tb = T_MIX
    nb = s // tb
    fwd = lambda bi, i: (bi, i, 0)
    bwd = lambda bi, i: (bi, nb - 1 - i, 0)
    full = lambda arr: pl.BlockSpec(arr.shape, lambda bi, i: (0,) * arr.ndim)
    blk = lambda w, m: pl.BlockSpec((1, tb, w), m)
    cs = [consts["exp_a0"], consts["exp_a1"], consts["masks0"], consts["masks1"], consts["tri0"], consts["tri1"],
          consts["bm"], consts["bm_f32"]]
    return pl.pallas_call(
        _mixer_a_kernel,
        grid=(b, nb),
        in_specs=[blk(A_QK, fwd), blk(A_QK, fwd), blk(A_VW, fwd), blk(LANES, fwd),
                  blk(A_QK, bwd), blk(A_QK, bwd), blk(A_VW, bwd), blk(LANES, bwd)] + [full(a) for a in cs],
        out_specs=[blk(A_VW, fwd), blk(A_VW, bwd)],
        out_shape=[jax.ShapeDtypeStruct((b, s, A_VW), MIX_OUT_DTYPE)] * 2,
        scratch_shapes=[pltpu.VMEM((2 * (A_HEADS // GROUP) * (GW // LANES), LANES, LANES), F32),
                        pltpu.VMEM((2, tb, 2 * A_VW), F32)],
        compiler_params=_params(("parallel", "arbitrary")),
        name="mixer_delta",
    )(q, k, v, ga, q, k, v, ga, *cs)


def _gla_local(q, k, v, g, masks, glacs, bm, bmk, bmk_f32, rev):
    cs = _cdot_hl(glacs, g)
    k_bf = _bf(k)
    attn = _dot_nt(_bf(q), _tile_rows(k_bf, GROUP) * bmk) * masks[M_EYE]
    yield
    b = cs[0:CHUNK]
    for li, s in enumerate(GLA_LEVELS):
        qe = cs[(1 + 2 * li) * CHUNK:(2 + 2 * li) * CHUNK]
        ke = cs[(2 + 2 * li) * CHUNK:(3 + 2 * li) * CHUNK]
        kl = _tile_rows(_bf(k * jnp.exp(ke)), GROUP) * bmk
        attn = attn + _dot_nt(_bf(q * jnp.exp(qe)), kl) * masks[_OFF_INDEX[s]]
        if li % 2 == 1:
            yield
    last = 0 if rev else CHUNK - 1
    blast = b[last:last + 1]
    qd_bf = _bf(q * jnp.exp(b))
    kd = k * jnp.exp(blast - b)
    cd = jnp.exp(blast)
    v_bf = _bf(v)
    o_intra = _dot(_bf(attn), _bd(v_bf, bm))
    upd = _dot_tn(v_bf, _bf(kd)) * bmk_f32
    yield
    return o_intra, qd_bf, upd, cd


def _gla_step(local, state_ref, d, o_ref, rows):
    o_intra, qd_bf, upd, cd = local
    state_t = state_ref[d]
    o_ref[0, rows, :] = (o_intra + _dot_nt(qd_bf, _bf(state_t))).astype(o_ref.dtype)
    state_ref[d] = state_t * cd + upd
    yield


def _cummax_rows(u, rev):
    n = u.shape[0]
    row = lax.broadcasted_iota(jnp.int32, u.shape, 0)
    x = u
    step = 1
    while step < n:
        if step % SUBLANES == 0:
            fill = jnp.full((step,) + u.shape[1:], -jnp.inf, u.dtype)
            sh = jnp.concatenate([x[step:], fill] if rev else [fill, x[:n - step]], axis=0)
        elif rev:
            sh = jnp.where(row < n - step, pltpu.roll(x, n - step, 0), -jnp.inf)
        else:
            sh = jnp.where(row >= step, pltpu.roll(x, step, 0), -jnp.inf)
        x = jnp.maximum(x, sh)
        step *= 2
    return x


def _mlstm_local(q, k, v, ip_e, lf_e, masks, tri, bm, bm_f32, rev):
    fc = _cdot_hl(tri, lf_e)
    q_bf = _bf(q)
    qk = _dot_nt(q_bf, _bd(_bf(k), bm))
    yield
    u = ip_e - fc
    cm = _cummax_rows(u, rev)
    dmax = fc + cm
    uj = jnp.sum(u * masks[M_EYE], axis=0, keepdims=True)
    p_bf = _bf(qk * jnp.exp(_where_mask(masks[M_INCL], uj - cm, -jnp.inf)))
    last = 0 if rev else CHUNK - 1
    flast = fc[last:last + 1]
    gmax = flast + cm[last:last + 1]
    v_bf = _bf(v)
    pv = _dot(p_bf, _bd(v_bf, bm))
    psum = _dot(p_bf, bm)
    kg = k * jnp.exp(flast + u - gmax)
    kg_bf = _bf(kg)
    pair_mask = bm_f32[0:LANES, 0:LANES]
    w = [_dot_tn(kg_bf[:, lanes], v_bf[:, lanes]) * pair_mask for lanes in _PAIRS]
    ksum = jnp.sum(kg, axis=0, keepdims=True)
    yield
    return q, q_bf, fc, dmax, flast, gmax, pv, psum, w, ksum


def _mlstm_step(local, c_ref, n_ref, mx_ref, d, o_ref, rows, bm):
    q, q_bf, fc, dmax, flast, gmax, pv, psum, w, ksum = local
    nrow, mrow = n_ref[d], mx_ref[d]
    cmats = [c_ref[d * len(_PAIRS) + p] for p in range(len(_PAIRS))]
    qc = jnp.concatenate([_dot(q_bf[:, lanes], _bf(cm)) for lanes, cm in zip(_PAIRS, cmats)], axis=1)
    qn = _dot(_bf(q * nrow), bm)
    yield
    m_inter = fc + mrow
    m_i = jnp.maximum(dmax, m_inter)
    s_inter = jnp.exp(m_inter - m_i)
    s_intra = jnp.exp(dmax - m_i)
    num = s_inter * qc + s_intra * pv
    den = s_inter * qn + s_intra * psum
    o_ref[0, rows, :] = (num / jnp.maximum(jnp.abs(den), jnp.exp(-m_i))).astype(o_ref.dtype)
    m_new = jnp.maximum(flast + mrow, gmax)
    a = jnp.exp(flast + mrow - m_new)
    sc = jnp.exp(gmax - m_new)
    for p, (lanes, cm) in enumerate(zip(_PAIRS, cmats)):
        c_ref[d * len(_PAIRS) + p] = a[:, lanes] * cm + sc[:, lanes] * w[p]
    n_ref[d] = a * nrow + sc * ksum
    mx_ref[d] = m_new
    yield


def _mixer_bc_kernel(bf_ref, bb_ref, cf_ref, cb_ref, wup_ref, bg_ref, e0_ref, e1_ref, big_ref, bfg_ref,
                     m0_ref, m1_ref, t0_ref, t1_ref, g0_ref, g1_ref, bm_ref, bmf_ref, bmk_ref, bmkf_ref,
                     obf_ref, obb_ref, ocf_ref, ocb_ref, sb_ref, gk_ref, c_ref, n_ref, mx_ref, ge_ref):
    tb = bf_ref.shape[1]
    n_chunks = tb // CHUNK

    @pl.when(pl.program_id(1) == 0)
    def _():
        sb_ref[...] = jnp.zeros_like(sb_ref)
        c_ref[...] = jnp.zeros_like(c_ref)
        n_ref[...] = jnp.zeros_like(n_ref)
        mx_ref[...] = jnp.zeros_like(mx_ref)

    lr_col = 2 * B_QK + 2 * B_VW
    gate_col = 2 * C_QK + 2 * C_VW
    for d, (pb_ref, pc_ref, e_ref) in enumerate(((bf_ref, cf_ref, e0_ref), (bb_ref, cb_ref, e1_ref))):
        l_hi, l_lo = _split2(pb_ref[0, :, lr_col:lr_col + LANES])
        w_hi, w_lo = _split2(wup_ref[d])
        zz = _dot(l_hi, w_hi) + _dot(l_hi, w_lo) + _dot(l_lo, w_hi) + bg_ref[d]
        gk_ref[d] = _log_sigmoid(zz) * (1.0 / B_TAU)
        gt = pc_ref[0, :, gate_col:gate_col + LANES]
        col = lax.broadcasted_iota(jnp.int32, gt.shape, 1)
        gates = jnp.where(col < 2 * C_HEADS, gt + big_ref[...], _log_sigmoid(gt + bfg_ref[...]))
        ge_ref[d] = _dotc_hl(gates, e_ref[...])
    bm = bm_ref[...]
    bm_f32 = bmf_ref[...]
    bmk = bmk_ref[...]
    bmk_f32 = bmkf_ref[...]
    dirs = ((bf_ref, cf_ref, obf_ref, ocf_ref, m0_ref, t0_ref, g0_ref),
            (bb_ref, cb_ref, obb_ref, ocb_ref, m1_ref, t1_ref, g1_ref))

    def body(it, carry):
        units = []
        for j in range(BC_CHUNKS_PER_ITER):
            for d in range(2):
                units.append((j, d, pl.ds(_chunk_rows(it * BC_CHUNKS_PER_ITER + j, n_chunks, d == 1), CHUNK)))
        gens = []
        for (j, d, rows) in units:
            pb_ref, pc_ref, _, _, m_ref, t_ref, g_ref = dirs[d]
            gens.append(_gla_local(pb_ref[0, rows, 0:B_QK] * (B_DK ** -0.5), pb_ref[0, rows, B_QK:2 * B_QK],
                                   pb_ref[0, rows, 2 * B_QK:2 * B_QK + B_VW], gk_ref[d, rows, :],
                                   m_ref, g_ref[...], bm, bmk, bmk_f32, d == 1))
            gens.append(_mlstm_local(pc_ref[0, rows, 0:C_QK], pc_ref[0, rows, C_QK:2 * C_QK] * (C_DK ** -0.5),
                                     pc_ref[0, rows, 2 * C_QK:2 * C_QK + C_VW],
                                     ge_ref[d, rows, 0:C_VW], ge_ref[d, rows, C_VW:2 * C_VW],
                                     m_ref, t_ref[...], bm, bm_f32, d == 1))
        local = _interleave(gens)
        for j in range(BC_CHUNKS_PER_ITER):
            steps = []
            for u, (ju, d, rows) in enumerate(units):
                if ju == j:
                    steps.append(_gla_step(local[2 * u], sb_ref, d, dirs[d][2], rows))
                    steps.append(_mlstm_step(local[2 * u + 1], c_ref, n_ref, mx_ref, d, dirs[d][3], rows, bm))
            _interleave(steps)
        return carry

    lax.fori_loop(0, n_chunks // BC_CHUNKS_PER_ITER, body, 0)


def _mixer_bc(pb, pc, wup_pad, bgla_rows, big_row, bfg_row, consts):
    b, s, _ = pb.shape
    tb = T_MIX
    nb = s // tb
    fwd = lambda bi, i: (bi, i, 0)
    bwd = lambda bi, i: (bi, nb - 1 - i, 0)
    full = lambda arr: pl.BlockSpec(arr.shape, lambda bi, i: (0,) * arr.ndim)
    blk = lambda w, m: pl.BlockSpec((1, tb, w), m)
    cs = [wup_pad, bgla_rows, consts["exp_c0"], consts["exp_c1"], big_row, bfg_row,
          consts["masks0"], consts["masks1"], consts["tri0"], consts["tri1"], consts["glacs0"], consts["glacs1"],
          consts["bm"], consts["bm_f32"], consts["bmk"], consts["bmk_f32"]]
    return pl.pallas_call(
        _mixer_bc_kernel,
        grid=(b, nb),
        in_specs=[blk(W_B, fwd), blk(W_B, bwd), blk(W_C, fwd), blk(W_C, bwd)] + [full(a) for a in cs],
        out_specs=[blk(B_VW, fwd), blk(B_VW, bwd), blk(C_VW, fwd), blk(C_VW, bwd)],
        out_shape=[jax.ShapeDtypeStruct((b, s, B_VW), MIX_OUT_DTYPE)] * 2
        + [jax.ShapeDtypeStruct((b, s, C_VW), MIX_OUT_DTYPE)] * 2,
        scratch_shapes=[pltpu.VMEM((2, B_VW, B_QK), F32), pltpu.VMEM((2, tb, B_QK), F32),
                        pltpu.VMEM((2 * len(_PAIRS), LANES, LANES), F32), pltpu.VMEM((2, 1, GW), F32),
                        pltpu.VMEM((2, 1, GW), F32), pltpu.VMEM((2, tb, 2 * C_VW), F32)],
        compiler_params=_params(("parallel", "arbitrary")),
        name="mixer_gla_mlstm",
    )(pb, pb, pc, pc, *cs)


def _outproj_kernel(x_ref, mod_ref, af_ref, ab_ref, z_ref, bf_ref, bb_ref, bg_ref, cf_ref, cb_ref, co_ref,
                    na_ref, nb_ref, nc_ref, w_ref, gpost_ref, ones_ref, o_ref):
    ones_bd = ones_ref[...]
    oa = af_ref[0].astype(F32) + ab_ref[0].astype(F32)
    ms = _seg_sum64(oa * oa, ones_bd) * (1.0 / A_DV)
    oa = oa * lax.rsqrt(ms + EPS) * na_ref[...] * _silu(z_ref[0].astype(F32))
    ones_q = ones_bd[0:B_VW, 0:B_VW]
    ob = bf_ref[0].astype(F32) + bb_ref[0].astype(F32)
    ms = _seg_sum64(ob * ob, ones_q) * (1.0 / B_DV)
    ob = ob * lax.rsqrt(ms + EPS) * nb_ref[...] * _silu(bg_ref[0])
    hc = _sigmoid(co_ref[0]) * (cf_ref[0].astype(F32) + cb_ref[0].astype(F32))
    mu = _seg_sum64(hc, ones_q) * (1.0 / C_DV)
    dv = hc - mu
    var = _seg_sum64(dv * dv, ones_q) * (1.0 / C_DV)
    oc = dv * lax.rsqrt(var + EPS) * nc_ref[...]
    y = (_dot(_bf(oa), w_ref[0:A_VW, :]) + _dot(_bf(ob), w_ref[A_VW:A_VW + B_VW, :])
         + _dot(_bf(oc), w_ref[A_VW + B_VW:, :]))
    gt1 = mod_ref[0][2:3]
    o_ref[0] = x_ref[0] + gt1 * _rms(y, gpost_ref[...])


def _outproj(x, mod, oaf, oab, z, obf, obb, pb, ocf, ocb, pc, na_row, nb_row, nc_row, w_out_bf, g_post, ones512):
    b, s, d = x.shape
    t = T_OUT
    tok = lambda w, j=0: pl.BlockSpec((1, t, w), lambda bi, i: (bi, i, j))
    full = lambda arr: pl.BlockSpec(arr.shape, lambda bi, i: (0,) * arr.ndim)
    bg_blk = (2 * B_QK + B_VW) // B_VW
    co_blk = (2 * C_QK + C_VW) // C_VW
    return pl.pallas_call(
        _outproj_kernel,
        grid=(b, s // t),
        in_specs=[tok(d), pl.BlockSpec((1, 6, d), lambda bi, i: (bi, 0, 0)),
                  tok(A_VW), tok(A_VW), tok(A_VW), tok(B_VW), tok(B_VW), tok(B_VW, bg_blk),
                  tok(C_VW), tok(C_VW), tok(C_VW, co_blk),
                  full(na_row), full(nb_row), full(nc_row), full(w_out_bf), full(g_post), full(ones512)],
        out_specs=tok(d),
        out_shape=jax.ShapeDtypeStruct((b, s, d), F32),
        compiler_params=_params(("parallel", "arbitrary")),
        name="outproj",
    )(x, mod, oaf, oab, z, obf, obb, pb, ocf, ocb, pc, na_row, nb_row, nc_row, w_out_bf, g_post, ones512)


def _ffn_kernel(xp_ref, x_ref, xn_ref, mod_ref, gpre_ref, wu_ref, cw_ref, cb_ref, wd_ref, gpost_ref,
                o_ref, h_ref, u_ref, acc_ref):
    i = pl.program_id(1)
    nt = pl.num_programs(1)
    f = pl.program_id(2)
    nf = pl.num_programs(2)
    t = x_ref.shape[1]
    n_slabs = wu_ref.shape[1] // (2 * LANES)
    mod = mod_ref[0]

    @pl.when(f == 0)
    def _():
        xe = jnp.concatenate([xp_ref[0], x_ref[0], xn_ref[0]], axis=0)
        h_ref[...] = _bf(_rms(xe, gpre_ref[...]) * (1.0 + mod[4:5]) + mod[3:4])
        acc_ref[...] = jnp.zeros_like(acc_ref)

    h = h_ref[...]
    lo_ok = (i > 0).astype(F32)
    hi_ok = (i < nt - 1).astype(F32)

    def up(j):
        cols = slice(j * 2 * LANES, (j + 1) * 2 * LANES)
        u = _dot(h, wu_ref[:, cols])
        u_ref[0:SUBLANES, cols] = u[0:SUBLANES] * lo_ok
        u_ref[SUBLANES:t + SUBLANES, cols] = u[SUBLANES:t + SUBLANES]
        u_ref[t + SUBLANES:t + 2 * SUBLANES, cols] = u[t + SUBLANES:] * hi_ok

    def gated(j):
        cols = slice(j * 2 * LANES, (j + 1) * 2 * LANES)
        cw = cw_ref[:, cols]
        cv = (cw[0:1] * u_ref[pl.ds(SUBLANES - 1, t), cols] + cw[1:2] * u_ref[pl.ds(SUBLANES, t), cols]
              + cw[2:3] * u_ref[pl.ds(SUBLANES + 1, t), cols] + cb_ref[:, cols])
        gate = cv[:, 0:LANES]
        return _bf(0.5 * gate * (1.0 + lax.erf(gate * (2.0 ** -0.5))) * cv[:, LANES:])

    def down(j0, j1):
        act = jnp.concatenate([gated(j) for j in range(j0, j1)], axis=1)
        acc_ref[...] += _dot(act, wd_ref[j0 * LANES:j1 * LANES, :])

    for j in range(min(FFN_LAG, n_slabs)):
        up(j)
    for j0 in range(0, n_slabs, 2):
        for j in range(j0 + FFN_LAG, min(j0 + FFN_LAG + 2, n_slabs)):
            up(j)
        down(j0, min(j0 + 2, n_slabs))

    @pl.when(f == nf - 1)
    def _():
        o_ref[0] = x_ref[0] + mod[5:6] * _rms(acc_ref[...], gpost_ref[...])


def _pair_gate_value(a):
    lead = a.shape[:-1]
    g = a[..., :D_FF].reshape(lead + (D_FF // LANES, 1, LANES))
    v = a[..., D_FF:].reshape(lead + (D_FF // LANES, 1, LANES))
    return jnp.concatenate([g, v], axis=-2).reshape(lead + (2 * D_FF,))


def _ffn(x, mod, g_pre, w_up_pair_bf, conv_w_pair, conv_b_pair, w_down_bf, g_post):
    b, s, d = x.shape
    t = T_FFN
    ft = F_TILE
    nf = D_FF // ft
    tb = t // SUBLANES
    nblk8 = s // SUBLANES
    full = lambda shape: pl.BlockSpec(shape, lambda bi, i, f: (0,) * len(shape))
    return pl.pallas_call(
        _ffn_kernel,
        grid=(b, s // t, nf),
        in_specs=[pl.BlockSpec((1, SUBLANES, d), lambda bi, i, f: (bi, jnp.maximum(i * tb - 1, 0), 0)),
                  pl.BlockSpec((1, t, d), lambda bi, i, f: (bi, i, 0)),
                  pl.BlockSpec((1, SUBLANES, d), lambda bi, i, f: (bi, jnp.minimum((i + 1) * tb, nblk8 - 1), 0)),
                  pl.BlockSpec((1, 6, d), lambda bi, i, f: (bi, 0, 0)),
                  full((1, d)),
                  pl.BlockSpec((d, 2 * ft), lambda bi, i, f: (0, f)),
                  pl.BlockSpec((3, 2 * ft), lambda bi, i, f: (0, f)),
                  pl.BlockSpec((1, 2 * ft), lambda bi, i, f: (0, f)),
                  pl.BlockSpec((ft, d), lambda bi, i, f: (f, 0)),
                  full((1, d))],
        out_specs=pl.BlockSpec((1, t, d), lambda bi, i, f: (bi, i, 0)),
        out_shape=jax.ShapeDtypeStruct((b, s, d), F32),
        scratch_shapes=[pltpu.VMEM((t + 2 * SUBLANES, d), BF16),
                        pltpu.VMEM((t + 2 * SUBLANES, 2 * ft), F32),
                        pltpu.VMEM((t, d), F32)],
        compiler_params=_params(("parallel", "arbitrary", "arbitrary")),
        name="ffn",
    )(x, x, x, mod, g_pre, w_up_pair_bf, conv_w_pair, conv_b_pair, w_down_bf, g_post)


def _pad_cols(w, sections):
    parts, start = [], 0
    for width, padded in sections:
        parts.append(w[:, start:start + width])
        if padded > width:
            parts.append(jnp.zeros((w.shape[0], padded - width), w.dtype))
        start += width
    return jnp.concatenate(parts, axis=1)


def _row(v, width=None):
    v = v.reshape(1, -1).astype(F32)
    if width is not None and v.shape[1] < width:
        v = jnp.pad(v, ((0, 0), (0, width - v.shape[1])))
    return v


def kernel(x_prompt, x_sample, c_prompt, c_sample, w_ada, b_ada, g_pre_mix, w_in, conv_a, a_log, dt_bias, norm_a,
           w_gla_up, b_gla, norm_b, bias_ig, bias_fg, norm_c, w_out, g_post_mix, g_pre_ffn, w_up, conv_ffn,
           b_conv_ffn, w_down, g_post_ffn):
    consts = _constants()
    depth = w_ada.shape[0]
    d = D_MODEL
    n_prompt = x_prompt.shape[0]
    c_all = jnp.concatenate([c_prompt, c_sample], axis=0)
    xs = [x_prompt, x_sample]
    a_in = A_CONV_CH + A_VW + 4 * A_HEADS
    b_in = 2 * B_QK + 2 * B_VW + 2 * B_RANK
    c_in = 2 * C_QK + 2 * C_VW + 4 * C_HEADS
    for l in range(depth):
        mod_all = _mod(c_all, w_ada[l], b_ada[l])
        mods = [mod_all[:n_prompt].reshape(-1, 6, d), mod_all[n_prompt:].reshape(-1, 6, d)]
        w_in_p = _pad_cols(w_in[l].astype(BF16), [(a_in, A_CONV_CH + W_REST_A), (b_in, W_B), (c_in, W_C)])
        w_out_bf = w_out[l].astype(BF16)
        w_up_bf = _pair_gate_value(w_up[l].astype(BF16))
        conv_ffn_pair = _pair_gate_value(conv_ffn[l])
        b_ffn_pair = _pair_gate_value(_row(b_conv_ffn[l]))
        w_down_bf = w_down[l].astype(BF16)
        alog_row = _row(a_log[l], LANES)
        dtb_row = _row(dt_bias[l], LANES)
        wup_pad = jnp.zeros((2, LANES, B_QK), F32)
        for dd in range(2):
            wup_pad = wup_pad.at[dd, dd * B_RANK:(dd + 1) * B_RANK, :].set(w_gla_up[l, dd])
        bgla_rows = b_gla[l].reshape(2, 1, B_QK).astype(F32)
        big_row = _row(bias_ig[l], LANES)
        bfg_row = jnp.pad(bias_fg[l].reshape(1, -1).astype(F32), ((0, 0), (2 * C_HEADS, LANES - 4 * C_HEADS)))
        na_row = _row(jnp.tile(norm_a[l], A_HEADS))
        nb_row = _row(jnp.tile(norm_b[l], B_HEADS))
        nc_row = _row(norm_c[l])
        for gi in range(2):
            x, mod = xs[gi], mods[gi]
            q, k, v, z, ga, pb, pc = _inproj(x, mod, _row(g_pre_mix[l]), w_in_p, conv_a[l], consts["ones512"],
                                             alog_row, dtb_row)
            oaf, oab = _mixer_a(q, k, v, ga, consts)
            obf, obb, ocf, ocb = _mixer_bc(pb, pc, wup_pad, bgla_rows, big_row, bfg_row, consts)
            x = _outproj(x, mod, oaf, oab, z, obf, obb, pb, ocf, ocb, pc, na_row, nb_row, nc_row, w_out_bf,
                         _row(g_post_mix[l]), consts["ones512"])
            x = _ffn(x, mod, _row(g_pre_ffn[l]), w_up_bf, conv_ffn_pair, b_ffn_pair, w_down_bf,
                     _row(g_post_ffn[l]))
            xs[gi] = x
    return (xs[0], xs[1])
```

```python
import functools
import math

import numpy as np
import jax
import jax.numpy as jnp
from jax import lax
from jax.experimental import pallas as pl
from jax.experimental.pallas import tpu as pltpu

F32 = jnp.float32
BF16 = jnp.bfloat16

D_MODEL = 1024
A_HEADS, A_DK, A_DV = 8, 64, 64
B_HEADS, B_DK, B_DV, B_RANK, B_TAU = 4, 32, 64, 16, 16.0
C_HEADS, C_DK, C_DV = 4, 64, 64
D_FF = 2688
EPS = 1e-6

A_QK = A_HEADS * A_DK
A_VW = A_HEADS * A_DV
A_CONV_CH = 2 * A_QK + A_VW
B_QK = B_HEADS * B_DK
B_VW = B_HEADS * B_DV
C_QK = C_HEADS * C_DK
C_VW = C_HEADS * C_DV

LANES = 128
SUBLANES = 8
CHUNK = 64
GROUP = 4
GW = GROUP * 64

W_REST_A = A_VW + LANES
W_B = 2 * B_QK + 2 * B_VW + LANES
W_C = 2 * C_QK + 2 * C_VW + LANES
W_IN_PAD = A_CONV_CH + W_REST_A + W_B + W_C

T_PROJ = 512
T_MIX = 512
T_OUT = 1024
T_FFN = 1024
F_TILE = 896
FFN_LAG = 4
VMEM_LIMIT = 56 * 1024 * 1024
MIX_OUT_DTYPE = BF16
A_CHUNKS_PER_ITER = 8
A_WAVE = 4
BC_CHUNKS_PER_ITER = 4


def _dot(a, b):
    return jnp.dot(a, b, preferred_element_type=F32)


def _dot_nt(a, b):
    return lax.dot_general(a, b, (((1,), (1,)), ((), ())), preferred_element_type=F32)


def _dot_tn(a, b):
    return lax.dot_general(a, b, (((0,), (0,)), ((), ())), preferred_element_type=F32)


def _bf(x):
    return x.astype(BF16)


def _split2(x):
    hi = _bf(x)
    lo = _bf(x - hi.astype(F32))
    return hi, lo


def _cdot_hl(c, x):
    hi, lo = _split2(x)
    return _dot(c, hi) + _dot(c, lo)


def _dotc_hl(x, c):
    hi, lo = _split2(x)
    return _dot(hi, c) + _dot(lo, c)


def _tile_rows(x, n):
    return jnp.concatenate([x] * n, axis=0)


def _silu(x):
    return x * (1.0 / (1.0 + jnp.exp(-x)))


def _sigmoid(x):
    return 1.0 / (1.0 + jnp.exp(-x))


def _softplus(x):
    return jnp.maximum(x, 0.0) + jnp.log(1.0 + jnp.exp(-jnp.abs(x)))


def _log_sigmoid(x):
    return -_softplus(-x)


def _rms(x, g_row):
    return x * lax.rsqrt(jnp.mean(x * x, axis=-1, keepdims=True) + EPS) * g_row


def _params(sem):
    return pltpu.CompilerParams(dimension_semantics=sem, vmem_limit_bytes=VMEM_LIMIT)


def _dir_masks(rev):
    i = np.arange(CHUNK)[:, None]
    j = np.arange(CHUNK)[None, :]
    if rev:
        i, j = CHUNK - 1 - i, CHUNK - 1 - j
    incl = j <= i
    strict = j < i
    rows = [incl, strict, np.eye(CHUNK, dtype=bool), (i // 8 == j // 8) & strict]
    for s in (8, 16, 32, 1, 2, 4):
        rows.append((i // (2 * s) == j // (2 * s)) & (i // s == j // s + 1))
    return np.stack([np.tile(r, (1, GROUP)) for r in rows]).astype(np.float32), incl.astype(np.float32)


M_INCL, M_STRICT, M_EYE, M_BLK8, M_OFF8, M_OFF16, M_OFF32, M_OFF1, M_OFF2, M_OFF4 = range(10)
_OFF_INDEX = {8: M_OFF8, 16: M_OFF16, 32: M_OFF32, 1: M_OFF1, 2: M_OFF2, 4: M_OFF4}


def _gla_cumsum_mats(rev):
    i = np.arange(CHUNK)[:, None]
    t = np.arange(CHUNK)[None, :]
    mats = []
    for s in (32, 16, 8, 4, 2, 1):
        blk = (i // (2 * s)) * 2 * s
        upper = (i % (2 * s)) >= s
        csq = upper & (t >= blk + s) & (t <= i)
        csk = (~upper) & (t >= i + 1) & (t <= blk + s - 1)
        if rev:
            csq, csk = csq[::-1, ::-1], csk[::-1, ::-1]
        mats += [csq, csk]
    return mats


GLA_LEVELS = (32, 16, 8, 4, 2, 1)


def _constants():
    c = {}
    for d, rev in enumerate((False, True)):
        masks, tri = _dir_masks(rev)
        c[f"masks{d}"] = jnp.asarray(masks)
        c[f"tri{d}"] = jnp.asarray(tri, BF16)
        gl = np.concatenate([tri] + [m.astype(np.float32) for m in _gla_cumsum_mats(rev)], axis=0)
        c[f"glacs{d}"] = jnp.asarray(gl, BF16)
    r = np.arange(GW)
    c["bm"] = jnp.asarray((r[:, None] // 64 == r[None, :] // 64), BF16)
    c["bm_f32"] = jnp.asarray((r[:, None] // 64 == r[None, :] // 64), F32)
    c["bmk"] = jnp.asarray((r[:, None] // 64 == np.arange(B_QK)[None, :] // B_DK), BF16)
    c["bmk_f32"] = jnp.asarray((r[:, None] // 64 == np.arange(B_QK)[None, :] // B_DK), F32)
    r5 = np.arange(512)
    c["ones512"] = jnp.asarray((r5[:, None] // 64 == r5[None, :] // 64), BF16)
    for d in range(2):
        ex = np.zeros((LANES, 2 * A_VW), np.float32)
        for q in range(2):
            for h in range(A_HEADS):
                ex[16 * q + 8 * d + h, q * A_VW + h * 64: q * A_VW + (h + 1) * 64] = 1.0
        c[f"exp_a{d}"] = jnp.asarray(ex, BF16)
    for d in range(2):
        ex = np.zeros((LANES, 2 * C_VW), np.float32)
        for q in range(2):
            for h in range(C_HEADS):
                ex[8 * q + 4 * d + h, q * C_VW + h * 64: q * C_VW + (h + 1) * 64] = 1.0
        c[f"exp_c{d}"] = jnp.asarray(ex, BF16)
    return c


def _mod_kernel(c_ref, w_ref, b_ref, o_ref):
    c = c_ref[...]
    s = _silu(c)
    s_hi, s_lo = _split2(s)
    w = w_ref[...]
    w_hi, w_lo = _split2(w)
    o_ref[...] = _dot(s_hi, w_hi) + _dot(s_hi, w_lo) + _dot(s_lo, w_hi) + b_ref[...]


def _mod(c_all, w_ada_l, b_ada_l):
    n, d = c_all.shape
    n6 = w_ada_l.shape[1]
    tn = 1536
    return pl.pallas_call(
        _mod_kernel,
        grid=(n6 // tn,),
        in_specs=[pl.BlockSpec((n, d), lambda j: (0, 0)),
                  pl.BlockSpec((d, tn), lambda j: (0, j)),
                  pl.BlockSpec((1, tn), lambda j: (0, j))],
        out_specs=pl.BlockSpec((n, tn), lambda j: (0, j)),
        out_shape=jax.ShapeDtypeStruct((n, n6), F32),
        compiler_params=_params(("arbitrary",)),
        name="adaln_mod",
    )(c_all, w_ada_l, b_ada_l.reshape(1, n6))


def _seg_sum64(x, ones_bd):
    return _dot(_bf(x), ones_bd)


def _inproj_kernel(xp_ref, x_ref, xn_ref, mod_ref, g_ref, w_ref, conv_ref, ones_ref, alog_ref, dtb_ref,
                   q_ref, k_ref, v_ref, z_ref, ga_ref, pb_ref, pc_ref, scr_ref):
    i = pl.program_id(1)
    nt = pl.num_programs(1)
    t = x_ref.shape[1]
    mod = mod_ref[0]
    sh1, sc1 = mod[0:1], mod[1:2]
    xe = jnp.concatenate([xp_ref[0], x_ref[0], xn_ref[0]], axis=0)
    h = _bf(_rms(xe, g_ref[...]) * (1.0 + sc1) + sh1)
    lo_ok = (i > 0).astype(F32)
    hi_ok = (i < nt - 1).astype(F32)
    for part in range(A_CONV_CH // A_QK):
        cols = slice(part * A_QK, (part + 1) * A_QK)
        p = _dot(h, w_ref[:, cols])
        scr_ref[0:SUBLANES, cols] = p[0:SUBLANES] * lo_ok
        scr_ref[SUBLANES:t + SUBLANES, cols] = p[SUBLANES:t + SUBLANES]
        scr_ref[t + SUBLANES:t + 2 * SUBLANES, cols] = p[t + SUBLANES:] * hi_ok
    hm = h[SUBLANES:t + SUBLANES]
    rest = _dot(hm, w_ref[:, A_CONV_CH:])
    ones_bd = ones_ref[...]

    def conv_silu(cols):
        cw = conv_ref[:, cols]
        return _silu(cw[0:1] * scr_ref[pl.ds(SUBLANES - 1, t), cols] + cw[1:2] * scr_ref[pl.ds(SUBLANES, t), cols]
                     + cw[2:3] * scr_ref[pl.ds(SUBLANES + 1, t), cols])

    q = conv_silu(slice(0, A_QK))
    q_ref[0] = q * lax.rsqrt(_seg_sum64(q * q, ones_bd) + EPS) * (A_DK ** -0.5)
    k = conv_silu(slice(A_QK, 2 * A_QK))
    k_ref[0] = k * lax.rsqrt(_seg_sum64(k * k, ones_bd) + EPS)
    v_ref[0] = conv_silu(slice(2 * A_QK, A_CONV_CH))
    z_ref[0] = rest[:, 0:A_VW].astype(z_ref.dtype)
    gt = rest[:, A_VW:W_REST_A]
    col = lax.broadcasted_iota(jnp.int32, gt.shape, 1)
    gdec = -jnp.exp(alog_ref[...]) * _softplus(gt + dtb_ref[...])
    ga_ref[0] = jnp.where(col < 2 * A_HEADS, gdec, _sigmoid(gt))
    pb_ref[0] = rest[:, W_REST_A:W_REST_A + W_B]
    pc_ref[0] = rest[:, W_REST_A + W_B:]


def _inproj(x, mod, g_pre, w_in_p, conv_a, ones512, alog_row, dtb_row):
    b, s, d = x.shape
    t = T_PROJ
    nt = s // t
    tb = t // SUBLANES
    nblk8 = s // SUBLANES
    shapes = [(A_QK, F32), (A_QK, F32), (A_VW, F32), (A_VW, MIX_OUT_DTYPE), (LANES, F32), (W_B, F32), (W_C, F32)]
    full = lambda shape: pl.BlockSpec(shape, lambda bi, i: (0,) * len(shape))
    return pl.pallas_call(
        _inproj_kernel,
        grid=(b, nt),
        in_specs=[pl.BlockSpec((1, SUBLANES, d), lambda bi, i: (bi, jnp.maximum(i * tb - 1, 0), 0)),
                  pl.BlockSpec((1, t, d), lambda bi, i: (bi, i, 0)),
                  pl.BlockSpec((1, SUBLANES, d), lambda bi, i: (bi, jnp.minimum((i + 1) * tb, nblk8 - 1), 0)),
                  pl.BlockSpec((1, 6, d), lambda bi, i: (bi, 0, 0)),
                  full((1, d)), full((d, W_IN_PAD)), full((3, A_CONV_CH)), full((512, 512)),
                  full((1, LANES)), full((1, LANES))],
        out_specs=[pl.BlockSpec((1, t, w), lambda bi, i: (bi, i, 0)) for w, _ in shapes],
        out_shape=[jax.ShapeDtypeStruct((b, s, w), dt) for w, dt in shapes],
        scratch_shapes=[pltpu.VMEM((t + 2 * SUBLANES, A_CONV_CH), F32)],
        compiler_params=_params(("parallel", "arbitrary")),
        name="inproj",
    )(x, x, x, mod, g_pre, w_in_p, conv_a, ones512, alog_row, dtb_row)


def _bd(x_bf, bm_bf):
    zero = jnp.zeros((CHUNK, LANES), BF16)
    blocks = []
    for h in range(GROUP):
        t = (h * 64) // LANES
        lanes = slice(t * LANES, (t + 1) * LANES)
        piece = x_bf[:, lanes] * bm_bf[h * CHUNK:(h + 1) * CHUNK, lanes]
        blocks.append(jnp.concatenate([piece, zero] if t == 0 else [zero, piece], axis=1))
    return jnp.concatenate(blocks, axis=0)


def _hdot(a, b_bf_bd):
    return _dot(_bf(a), b_bf_bd)


_PAIRS = tuple(slice(p * LANES, (p + 1) * LANES) for p in range(GW // LANES))


def _chunk_rows(c, n_chunks, rev):
    idx = (n_chunks - 1 - c) if rev else c
    return pl.multiple_of(idx * CHUNK, CHUNK)


def _where_mask(m, x, other):
    return jnp.where(m > 0.5, x, other)


def _otimes(a, b, bm):
    return _dot(_bf(a), _bd(_bf(b), bm))


def _delta_local(q, k, v, g_e, b_e, masks, tri, bm, rev):
    gam = _cdot_hl(tri, g_e)
    kb = k * b_e
    kq = _dot_nt(_bf(jnp.concatenate([kb, q], axis=0)), _bd(_bf(k), bm))
    yield
    gj = jnp.sum(gam * masks[M_EYE], axis=0, keepdims=True)
    decay = jnp.exp(_where_mask(masks[M_INCL], gam - gj, -jnp.inf))
    eg = jnp.exp(gam)
    m = kq[0:CHUNK] * decay * masks[M_STRICT]
    aqk = kq[CHUNK:] * decay
    last = 0 if rev else CHUNK - 1
    glast = gam[last:last + 1]
    qd = q * eg
    kd = k * jnp.exp(glast - gam)
    cd = jnp.exp(glast)
    d = m * masks[M_BLK8]
    p1 = _otimes(d, d, bm)
    yield
    x = masks[M_EYE] - d
    both = _otimes(jnp.concatenate([p1, x], axis=0), p1, bm)
    p2 = both[0:CHUNK]
    x = x + both[CHUNK:]
    yield
    x = x + _otimes(x, p2, bm)
    yield
    for s in (8, 16, 32):
        e = m * masks[_OFF_INDEX[s]]
        ex = _otimes(e, x, bm)
        yield
        x = x - _otimes(x, ex, bm)
        yield
    wv = _otimes(x, v * b_e, bm)
    kc = _otimes(x, kb * eg, bm)
    yield
    return wv, kc, aqk, qd, kd, cd


def _delta_step(local, state_ref, idx, o_ref, rows, cols, bm, bm_f32):
    wv, kc, aqk, qd, kd, cd = local
    lhs = _bf(jnp.concatenate([kc, qd], axis=0))
    pairs = [slice(p * LANES, (p + 1) * LANES) for p in range(GW // LANES)]
    states = [state_ref[idx * len(pairs) + p] for p in range(len(pairs))]
    ks_qs = jnp.concatenate([_dot(lhs[:, lanes], _bf(st)) for lanes, st in zip(pairs, states)], axis=1)
    yield
    vnew_bf = _bf(wv - ks_qs[0:CHUNK])
    o_ref[0, rows, cols] = (ks_qs[CHUNK:] + _dot(_bf(aqk), _bd(vnew_bf, bm))).astype(o_ref.dtype)
    kd_bf = _bf(kd)
    pair_mask = bm_f32[0:LANES, 0:LANES]
    for p, (lanes, st) in enumerate(zip(pairs, states)):
        state_ref[idx * len(pairs) + p] = (st * cd[:, lanes]
                                           + _dot_tn(kd_bf[:, lanes], vnew_bf[:, lanes]) * pair_mask)
    yield


def _interleave(gens):
    out = [None] * len(gens)
    live = list(enumerate(gens))
    while live:
        nxt = []
        for i, g in live:
            try:
                next(g)
                nxt.append((i, g))
            except StopIteration as stop:
                out[i] = stop.value
        live = nxt
    return out


def _mixer_a_kernel(qf_ref, kf_ref, vf_ref, gf_ref, qb_ref, kb_ref, vb_ref, gb_ref,
                    e0_ref, e1_ref, m0_ref, m1_ref, t0_ref, t1_ref, bm_ref, bmf_ref,
                    of_ref, ob_ref, state_ref, ge_ref):
    tb = qf_ref.shape[1]
    n_chunks = tb // CHUNK
    n_groups = A_HEADS // GROUP

    @pl.when(pl.program_id(1) == 0)
    def _():
        state_ref[...] = jnp.zeros_like(state_ref)

    for d, (g_ref, e_ref) in enumerate(((gf_ref, e0_ref), (gb_ref, e1_ref))):
        ge_ref[d, :, 0:A_VW] = _dotc_hl(g_ref[0], e_ref[:, 0:A_VW])
        ge_ref[d, :, A_VW:] = _dot(_bf(g_ref[0]), e_ref[:, A_VW:])
    bm = bm_ref[...]
    bm_f32 = bmf_ref[...]
    dirs = ((qf_ref, kf_ref, vf_ref, of_ref, m0_ref, t0_ref), (qb_ref, kb_ref, vb_ref, ob_ref, m1_ref, t1_ref))

    def local_gen(unit):
        j, d, hg, rows, cols = unit
        return _delta_local(dirs[d][0][0, rows, cols], dirs[d][1][0, rows, cols], dirs[d][2][0, rows, cols],
                            ge_ref[d, rows, cols], ge_ref[d, rows, A_VW + hg * GW:A_VW + (hg + 1) * GW],
                            dirs[d][4], dirs[d][5][...], bm, d == 1)

    def steps(units, local):
        for j in sorted({u[0] for u in units}):
            live = [_delta_step(local[i], state_ref, d * n_groups + hg, dirs[d][3], rows, cols, bm, bm_f32)
                    for i, (ju, d, hg, rows, cols) in enumerate(units) if ju == j]
            while live:
                nxt = []
                for g in live:
                    try:
                        next(g)
                        nxt.append(g)
                    except StopIteration:
                        pass
                live = nxt
                yield

    def body(it, carry):
        prev = None
        for w in range(A_CHUNKS_PER_ITER // A_WAVE):
            units = []
            for j in range(w * A_WAVE, (w + 1) * A_WAVE):
                for d in range(2):
                    rows = pl.ds(_chunk_rows(it * A_CHUNKS_PER_ITER + j, n_chunks, d == 1), CHUNK)
                    for hg in range(n_groups):
                        units.append((j, d, hg, rows, slice(hg * GW, (hg + 1) * GW)))
            gens = [local_gen(u) for u in units]
            if prev is not None:
                gens.append(steps(*prev))
            prev = (units, _interleave(gens)[:len(units)])
        _interleave([steps(*prev)])
        return carry

    lax.fori_loop(0, n_chunks // A_CHUNKS_PER_ITER, body, 0)


def _mixer_a(q, k, v, ga, consts):
    b, s, _ = q.shape
    tb = T_MIX
    nb = s // tb
    fwd = lambda bi, i: (bi, i, 0)
    bwd = lambda bi, i: (bi, nb - 1 - i, 0)
    full = lambda arr: pl.BlockSpec(arr.shape, lambda bi, i: (0,) * arr.ndim)
    blk = lambda w, m: pl.BlockSpec((1, tb, w), m)
    cs = [consts["exp_a0"], consts["exp_a1"], consts["masks0"], consts["masks1"], consts["tri0"], consts["tri1"],
          consts["bm"], consts["bm_f32"]]
    return pl.pallas_call(
        _mixer_a_kernel,
        grid=(b, nb),
        in_specs=[blk(A_QK, fwd), blk(A_QK, fwd), blk(A_VW, fwd), blk(LANES, fwd),
                  blk(A_QK, bwd), blk(A_QK, bwd), blk(A_VW, bwd), blk(LANES, bwd)] + [full(a) for a in cs],
        out_specs=[blk(A_VW, fwd), blk(A_VW, bwd)],
        out_shape=[jax.ShapeDtypeStruct((b, s, A_VW), MIX_OUT_DTYPE)] * 2,
        scratch_shapes=[pltpu.VMEM((2 * (A_HEADS // GROUP) * (GW // LANES), LANES, LANES), F32),
                        pltpu.VMEM((2, tb, 2 * A_VW), F32)],
        compiler_params=_params(("parallel", "arbitrary")),
        name="mixer_delta",
    )(q, k, v, ga, q, k, v, ga, *cs)


def _gla_local(q, k, v, g, masks, glacs, bm, bmk, bmk_f32, rev):
    cs = _cdot_hl(glacs, g)
    k_bf = _bf(k)
    attn = _dot_nt(_bf(q), _tile_rows(k_bf, GROUP) * bmk) * masks[M_EYE]
    yield
    b = cs[0:CHUNK]
    for li, s in enumerate(GLA_LEVELS):
        qe = cs[(1 + 2 * li) * CHUNK:(2 + 2 * li) * CHUNK]
        ke = cs[(2 + 2 * li) * CHUNK:(3 + 2 * li) * CHUNK]
        kl = _tile_rows(_bf(k * jnp.exp(ke)), GROUP) * bmk
        attn = attn + _dot_nt(_bf(q * jnp.exp(qe)), kl) * masks[_OFF_INDEX[s]]
        if li % 2 == 1:
            yield
    last = 0 if rev else CHUNK - 1
    blast = b[last:last + 1]
    qd_bf = _bf(q * jnp.exp(b))
    kd = k * jnp.exp(blast - b)
    cd = jnp.exp(blast)
    v_bf = _bf(v)
    o_intra = _dot(_bf(attn), _bd(v_bf, bm))
    upd = _dot_tn(v_bf, _bf(kd)) * bmk_f32
    yield
    return o_intra, qd_bf, upd, cd


def _gla_step(local, state_ref, d, o_ref, rows):
    o_intra, qd_bf, upd, cd = local
    state_t = state_ref[d]
    o_ref[0, rows, :] = (o_intra + _dot_nt(qd_bf, _bf(state_t))).astype(o_ref.dtype)
    state_ref[d] = state_t * cd + upd
    yield


def _cummax_rows(u, rev):
    n = u.shape[0]
    row = lax.broadcasted_iota(jnp.int32, u.shape, 0)
    x = u
    step = 1
    while step < n:
        if step % SUBLANES == 0:
            fill = jnp.full((step,) + u.shape[1:], -jnp.inf, u.dtype)
            sh = jnp.concatenate([x[step:], fill] if rev else [fill, x[:n - step]], axis=0)
        elif rev:
            sh = jnp.where(row < n - step, pltpu.roll(x, n - step, 0), -jnp.inf)
        else:
            sh = jnp.where(row >= step, pltpu.roll(x, step, 0), -jnp.inf)
        x = jnp.maximum(x, sh)
        step *= 2
    return x


def _mlstm_local(q, k, v, ip_e, lf_e, masks, tri, bm, bm_f32, rev):
    fc = _cdot_hl(tri, lf_e)
    q_bf = _bf(q)
    qk = _dot_nt(q_bf, _bd(_bf(k), bm))
    yield
    u = ip_e - fc
    cm = _cummax_rows(u, rev)
    dmax = fc + cm
    uj = jnp.sum(u * masks[M_EYE], axis=0, keepdims=True)
    p_bf = _bf(qk * jnp.exp(_where_mask(masks[M_INCL], uj - cm, -jnp.inf)))
    last = 0 if rev else CHUNK - 1
    flast = fc[last:last + 1]
    gmax = flast + cm[last:last + 1]
    v_bf = _bf(v)
    pv = _dot(p_bf, _bd(v_bf, bm))
    psum = _dot(p_bf, bm)
    kg = k * jnp.exp(flast + u - gmax)
    kg_bf = _bf(kg)
    pair_mask = bm_f32[0:LANES, 0:LANES]
    w = [_dot_tn(kg_bf[:, lanes], v_bf[:, lanes]) * pair_mask for lanes in _PAIRS]
    ksum = jnp.sum(kg, axis=0, keepdims=True)
    yield
    return q, q_bf, fc, dmax, flast, gmax, pv, psum, w, ksum


def _mlstm_step(local, c_ref, n_ref, mx_ref, d, o_ref, rows, bm):
    q, q_bf, fc, dmax, flast, gmax, pv, psum, w, ksum = local
    nrow, mrow = n_ref[d], mx_ref[d]
    cmats = [c_ref[d * len(_PAIRS) + p] for p in range(len(_PAIRS))]
    qc = jnp.concatenate([_dot(q_bf[:, lanes], _bf(cm)) for lanes, cm in zip(_PAIRS, cmats)], axis=1)
    qn = _dot(_bf(q * nrow), bm)
    yield
    m_inter = fc + mrow
    m_i = jnp.maximum(dmax, m_inter)
    s_inter = jnp.exp(m_inter - m_i)
    s_intra = jnp.exp(dmax - m_i)
    num = s_inter * qc + s_intra * pv
    den = s_inter * qn + s_intra * psum
    o_ref[0, rows, :] = (num / jnp.maximum(jnp.abs(den), jnp.exp(-m_i))).astype(o_ref.dtype)
    m_new = jnp.maximum(flast + mrow, gmax)
    a = jnp.exp(flast + mrow - m_new)
    sc = jnp.exp(gmax - m_new)
    for p, (lanes, cm) in enumerate(zip(_PAIRS, cmats)):
        c_ref[d * len(_PAIRS) + p] = a[:, lanes] * cm + sc[:, lanes] * w[p]
    n_ref[d] = a * nrow + sc * ksum
    mx_ref[d] = m_new
    yield


def _mixer_bc_kernel(bf_ref, bb_ref, cf_ref, cb_ref, wup_ref, bg_ref, e0_ref, e1_ref, big_ref, bfg_ref,
                     m0_ref, m1_ref, t0_ref, t1_ref, g0_ref, g1_ref, bm_ref, bmf_ref, bmk_ref, bmkf_ref,
                     obf_ref, obb_ref, ocf_ref, ocb_ref, sb_ref, gk_ref, c_ref, n_ref, mx_ref, ge_ref):
    tb = bf_ref.shape[1]
    n_chunks = tb // CHUNK

    @pl.when(pl.program_id(1) == 0)
    def _():
        sb_ref[...] = jnp.zeros_like(sb_ref)
        c_ref[...] = jnp.zeros_like(c_ref)
        n_ref[...] = jnp.zeros_like(n_ref)
        mx_ref[...] = jnp.zeros_like(mx_ref)

    lr_col = 2 * B_QK + 2 * B_VW
    gate_col = 2 * C_QK + 2 * C_VW
    for d, (pb_ref, pc_ref, e_ref) in enumerate(((bf_ref, cf_ref, e0_ref), (bb_ref, cb_ref, e1_ref))):
        l_hi, l_lo = _split2(pb_ref[0, :, lr_col:lr_col + LANES])
        w_hi, w_lo = _split2(wup_ref[d])
        zz = _dot(l_hi, w_hi) + _dot(l_hi, w_lo) + _dot(l_lo, w_hi) + bg_ref[d]
        gk_ref[d] = _log_sigmoid(zz) * (1.0 / B_TAU)
        gt = pc_ref[0, :, gate_col:gate_col + LANES]
        col = lax.broadcasted_iota(jnp.int32, gt.shape, 1)
        gates = jnp.where(col < 2 * C_HEADS, gt + big_ref[...], _log_sigmoid(gt + bfg_ref[...]))
        ge_ref[d] = _dotc_hl(gates, e_ref[...])
    bm = bm_ref[...]
    bm_f32 = bmf_ref[...]
    bmk = bmk_ref[...]
    bmk_f32 = bmkf_ref[...]
    dirs = ((bf_ref, cf_ref, obf_ref, ocf_ref, m0_ref, t0_ref, g0_ref),
            (bb_ref, cb_ref, obb_ref, ocb_ref, m1_ref, t1_ref, g1_ref))

    def body(it, carry):
        units = []
        for j in range(BC_CHUNKS_PER_ITER):
            for d in range(2):
                units.append((j, d, pl.ds(_chunk_rows(it * BC_CHUNKS_PER_ITER + j, n_chunks, d == 1), CHUNK)))
        gens = []
        for (j, d, rows) in units:
            pb_ref, pc_ref, _, _, m_ref, t_ref, g_ref = dirs[d]
            gens.append(_gla_local(pb_ref[0, rows, 0:B_QK] * (B_DK ** -0.5), pb_ref[0, rows, B_QK:2 * B_QK],
                                   pb_ref[0, rows, 2 * B_QK:2 * B_QK + B_VW], gk_ref[d, rows, :],
                                   m_ref, g_ref[...], bm, bmk, bmk_f32, d == 1))
            gens.append(_mlstm_local(pc_ref[0, rows, 0:C_QK], pc_ref[0, rows, C_QK:2 * C_QK] * (C_DK ** -0.5),
                                     pc_ref[0, rows, 2 * C_QK:2 * C_QK + C_VW],
                                     ge_ref[d, rows, 0:C_VW], ge_ref[d, rows, C_VW:2 * C_VW],
                                     m_ref, t_ref[...], bm, bm_f32, d == 1))
        local = _interleave(gens)
        for j in range(BC_CHUNKS_PER_ITER):
            steps = []
            for u, (ju, d, rows) in enumerate(units):
                if ju == j:
                    steps.append(_gla_step(local[2 * u], sb_ref, d, dirs[d][2], rows))
                    steps.append(_mlstm_step(local[2 * u + 1], c_ref, n_ref, mx_ref, d, dirs[d][3], rows, bm))
            _interleave(steps)
        return carry

    lax.fori_loop(0, n_chunks // BC_CHUNKS_PER_ITER, body, 0)


def _mixer_bc(pb, pc, wup_pad, bgla_rows, big_row, bfg_row, consts):
    b, s, _ = pb.shape
    tb = T_MIX
    nb = s // tb
    fwd = lambda bi, i: (bi, i, 0)
    bwd = lambda bi, i: (bi, nb - 1 - i, 0)
    full = lambda arr: pl.BlockSpec(arr.shape, lambda bi, i: (0,) * arr.ndim)
    blk = lambda w, m: pl.BlockSpec((1, tb, w), m)
    cs = [wup_pad, bgla_rows, consts["exp_c0"], consts["exp_c1"], big_row, bfg_row,
          consts["masks0"], consts["masks1"], consts["tri0"], consts["tri1"], consts["glacs0"], consts["glacs1"],
          consts["bm"], consts["bm_f32"], consts["bmk"], consts["bmk_f32"]]
    return pl.pallas_call(
        _mixer_bc_kernel,
        grid=(b, nb),
        in_specs=[blk(W_B, fwd), blk(W_B, bwd), blk(W_C, fwd), blk(W_C, bwd)] + [full(a) for a in cs],
        out_specs=[blk(B_VW, fwd), blk(B_VW, bwd), blk(C_VW, fwd), blk(C_VW, bwd)],
        out_shape=[jax.ShapeDtypeStruct((b, s, B_VW), MIX_OUT_DTYPE)] * 2
        + [jax.ShapeDtypeStruct((b, s, C_VW), MIX_OUT_DTYPE)] * 2,
        scratch_shapes=[pltpu.VMEM((2, B_VW, B_QK), F32), pltpu.VMEM((2, tb, B_QK), F32),
                        pltpu.VMEM((2 * len(_PAIRS), LANES, LANES), F32), pltpu.VMEM((2, 1, GW), F32),
                        pltpu.VMEM((2, 1, GW), F32), pltpu.VMEM((2, tb, 2 * C_VW), F32)],
        compiler_params=_params(("parallel", "arbitrary")),
        name="mixer_gla_mlstm",
    )(pb, pb, pc, pc, *cs)


def _outproj_kernel(x_ref, mod_ref, af_ref, ab_ref, z_ref, bf_ref, bb_ref, bg_ref, cf_ref, cb_ref, co_ref,
                    na_ref, nb_ref, nc_ref, w_ref, gpost_ref, ones_ref, o_ref):
    ones_bd = ones_ref[...]
    oa = af_ref[0].astype(F32) + ab_ref[0].astype(F32)
    ms = _seg_sum64(oa * oa, ones_bd) * (1.0 / A_DV)
    oa = oa * lax.rsqrt(ms + EPS) * na_ref[...] * _silu(z_ref[0].astype(F32))
    ones_q = ones_bd[0:B_VW, 0:B_VW]
    ob = bf_ref[0].astype(F32) + bb_ref[0].astype(F32)
    ms = _seg_sum64(ob * ob, ones_q) * (1.0 / B_DV)
    ob = ob * lax.rsqrt(ms + EPS) * nb_ref[...] * _silu(bg_ref[0])
    hc = _sigmoid(co_ref[0]) * (cf_ref[0].astype(F32) + cb_ref[0].astype(F32))
    mu = _seg_sum64(hc, ones_q) * (1.0 / C_DV)
    dv = hc - mu
    var = _seg_sum64(dv * dv, ones_q) * (1.0 / C_DV)
    oc = dv * lax.rsqrt(var + EPS) * nc_ref[...]
    y = (_dot(_bf(oa), w_ref[0:A_VW, :]) + _dot(_bf(ob), w_ref[A_VW:A_VW + B_VW, :])
         + _dot(_bf(oc), w_ref[A_VW + B_VW:, :]))
    gt1 = mod_ref[0][2:3]
    o_ref[0] = x_ref[0] + gt1 * _rms(y, gpost_ref[...])


def _outproj(x, mod, oaf, oab, z, obf, obb, pb, ocf, ocb, pc, na_row, nb_row, nc_row, w_out_bf, g_post, ones512):
    b, s, d = x.shape
    t = T_OUT
    tok = lambda w, j=0: pl.BlockSpec((1, t, w), lambda bi, i: (bi, i, j))
    full = lambda arr: pl.BlockSpec(arr.shape, lambda bi, i: (0,) * arr.ndim)
    bg_blk = (2 * B_QK + B_VW) // B_VW
    co_blk = (2 * C_QK + C_VW) // C_VW
    return pl.pallas_call(
        _outproj_kernel,
        grid=(b, s // t),
        in_specs=[tok(d), pl.BlockSpec((1, 6, d), lambda bi, i: (bi, 0, 0)),
                  tok(A_VW), tok(A_VW), tok(A_VW), tok(B_VW), tok(B_VW), tok(B_VW, bg_blk),
                  tok(C_VW), tok(C_VW), tok(C_VW, co_blk),
                  full(na_row), full(nb_row), full(nc_row), full(w_out_bf), full(g_post), full(ones512)],
        out_specs=tok(d),
        out_shape=jax.ShapeDtypeStruct((b, s, d), F32),
        compiler_params=_params(("parallel", "arbitrary")),
        name="outproj",
    )(x, mod, oaf, oab, z, obf, obb, pb, ocf, ocb, pc, na_row, nb_row, nc_row, w_out_bf, g_post, ones512)


def _ffn_kernel(xp_ref, x_ref, xn_ref, mod_ref, gpre_ref, wu_ref, cw_ref, cb_ref, wd_ref, gpost_ref,
                o_ref, h_ref, u_ref, acc_ref):
    i = pl.program_id(1)
    nt = pl.num_programs(1)
    f = pl.program_id(2)
    nf = pl.num_programs(2)
    t = x_ref.shape[1]
    n_slabs = wu_ref.shape[1] // (2 * LANES)
    mod = mod_ref[0]

    @pl.when(f == 0)
    def _():
        xe = jnp.concatenate([xp_ref[0], x_ref[0], xn_ref[0]], axis=0)
        h_ref[...] = _bf(_rms(xe, gpre_ref[...]) * (1.0 + mod[4:5]) + mod[3:4])
        acc_ref[...] = jnp.zeros_like(acc_ref)

    h = h_ref[...]
    lo_ok = (i > 0).astype(F32)
    hi_ok = (i < nt - 1).astype(F32)

    def up(j):
        cols = slice(j * 2 * LANES, (j + 1) * 2 * LANES)
        u = _dot(h, wu_ref[:, cols])
        u_ref[0:SUBLANES, cols] = u[0:SUBLANES] * lo_ok
        u_ref[SUBLANES:t + SUBLANES, cols] = u[SUBLANES:t + SUBLANES]
        u_ref[t + SUBLANES:t + 2 * SUBLANES, cols] = u[t + SUBLANES:] * hi_ok

    def gated(j):
        cols = slice(j * 2 * LANES, (j + 1) * 2 * LANES)
        cw = cw_ref[:, cols]
        cv = (cw[0:1] * u_ref[pl.ds(SUBLANES - 1, t), cols] + cw[1:2] * u_ref[pl.ds(SUBLANES, t), cols]
              + cw[2:3] * u_ref[pl.ds(SUBLANES + 1, t), cols] + cb_ref[:, cols])
        gate = cv[:, 0:LANES]
        return _bf(0.5 * gate * (1.0 + lax.erf(gate * (2.0 ** -0.5))) * cv[:, LANES:])

    def down(j0, j1):
        act = jnp.concatenate([gated(j) for j in range(j0, j1)], axis=1)
        acc_ref[...] += _dot(act, wd_ref[j0 * LANES:j1 * LANES, :])

    for j in range(min(FFN_LAG, n_slabs)):
        up(j)
    for j0 in range(0, n_slabs, 2):
        for j in range(j0 + FFN_LAG, min(j0 + FFN_LAG + 2, n_slabs)):
            up(j)
        down(j0, min(j0 + 2, n_slabs))

    @pl.when(f == nf - 1)
    def _():
        o_ref[0] = x_ref[0] + mod[5:6] * _rms(acc_ref[...], gpost_ref[...])


def _pair_gate_value(a):
    lead = a.shape[:-1]
    g = a[..., :D_FF].reshape(lead + (D_FF // LANES, 1, LANES))
    v = a[..., D_FF:].reshape(lead + (D_FF // LANES, 1, LANES))
    return jnp.concatenate([g, v], axis=-2).reshape(lead + (2 * D_FF,))


def _ffn(x, mod, g_pre, w_up_pair_bf, conv_w_pair, conv_b_pair, w_down_bf, g_post):
    b, s, d = x.shape
    t = T_FFN
    ft = F_TILE
    nf = D_FF // ft
    tb = t // SUBLANES
    nblk8 = s // SUBLANES
    full = lambda shape: pl.BlockSpec(shape, lambda bi, i, f: (0,) * len(shape))
    return pl.pallas_call(
        _ffn_kernel,
        grid=(b, s // t, nf),
        in_specs=[pl.BlockSpec((1, SUBLANES, d), lambda bi, i, f: (bi, jnp.maximum(i * tb - 1, 0), 0)),
                  pl.BlockSpec((1, t, d), lambda bi, i, f: (bi, i, 0)),
                  pl.BlockSpec((1, SUBLANES, d), lambda bi, i, f: (bi, jnp.minimum((i + 1) * tb, nblk8 - 1), 0)),
                  pl.BlockSpec((1, 6, d), lambda bi, i, f: (bi, 0, 0)),
                  full((1, d)),
                  pl.BlockSpec((d, 2 * ft), lambda bi, i, f: (0, f)),
                  pl.BlockSpec((3, 2 * ft), lambda bi, i, f: (0, f)),
                  pl.BlockSpec((1, 2 * ft), lambda bi, i, f: (0, f)),
                  pl.BlockSpec((ft, d), lambda bi, i, f: (f, 0)),
                  full((1, d))],
        out_specs=pl.BlockSpec((1, t, d), lambda bi, i, f: (bi, i, 0)),
        out_shape=jax.ShapeDtypeStruct((b, s, d), F32),
        scratch_shapes=[pltpu.VMEM((t + 2 * SUBLANES, d), BF16),
                        pltpu.VMEM((t + 2 * SUBLANES, 2 * ft), F32),
                        pltpu.VMEM((t, d), F32)],
        compiler_params=_params(("parallel", "arbitrary", "arbitrary")),
        name="ffn",
    )(x, x, x, mod, g_pre, w_up_pair_bf, conv_w_pair, conv_b_pair, w_down_bf, g_post)


def _pad_cols(w, sections):
    parts, start = [], 0
    for width, padded in sections:
        parts.append(w[:, start:start + width])
        if padded > width:
            parts.append(jnp.zeros((w.shape[0], padded - width), w.dtype))
        start += width
    return jnp.concatenate(parts, axis=1)


def _row(v, width=None):
    v = v.reshape(1, -1).astype(F32)
    if width is not None and v.shape[1] < width:
        v = jnp.pad(v, ((0, 0), (0, width - v.shape[1])))
    return v


def kernel(x_prompt, x_sample, c_prompt, c_sample, w_ada, b_ada, g_pre_mix, w_in, conv_a, a_log, dt_bias, norm_a,
           w_gla_up, b_gla, norm_b, bias_ig, bias_fg, norm_c, w_out, g_post_mix, g_pre_ffn, w_up, conv_ffn,
           b_conv_ffn, w_down, g_post_ffn):
    consts = _constants()
    depth = w_ada.shape[0]
    d = D_MODEL
    n_prompt = x_prompt.shape[0]
    c_all = jnp.concatenate([c_prompt, c_sample], axis=0)
    xs = [x_prompt, x_sample]
    a_in = A_CONV_CH + A_VW + 4 * A_HEADS
    b_in = 2 * B_QK + 2 * B_VW + 2 * B_RANK
    c_in = 2 * C_QK + 2 * C_VW + 4 * C_HEADS
    for l in range(depth):
        mod_all = _mod(c_all, w_ada[l], b_ada[l])
        mods = [mod_all[:n_prompt].reshape(-1, 6, d), mod_all[n_prompt:].reshape(-1, 6, d)]
        w_in_p = _pad_cols(w_in[l].astype(BF16), [(a_in, A_CONV_CH + W_REST_A), (b_in, W_B), (c_in, W_C)])
        w_out_bf = w_out[l].astype(BF16)
        w_up_bf = _pair_gate_value(w_up[l].astype(BF16))
        conv_ffn_pair = _pair_gate_value(conv_ffn[l])
        b_ffn_pair = _pair_gate_value(_row(b_conv_ffn[l]))
        w_down_bf = w_down[l].astype(BF16)
        alog_row = _row(a_log[l], LANES)
        dtb_row = _row(dt_bias[l], LANES)
        wup_pad = jnp.zeros((2, LANES, B_QK), F32)
        for dd in range(2):
            wup_pad = wup_pad.at[dd, dd * B_RANK:(dd + 1) * B_RANK, :].set(w_gla_up[l, dd])
        bgla_rows = b_gla[l].reshape(2, 1, B_QK).astype(F32)
        big_row = _row(bias_ig[l], LANES)
        bfg_row = jnp.pad(bias_fg[l].reshape(1, -1).astype(F32), ((0, 0), (2 * C_HEADS, LANES - 4 * C_HEADS)))
        na_row = _row(jnp.tile(norm_a[l], A_HEADS))
        nb_row = _row(jnp.tile(norm_b[l], B_HEADS))
        nc_row = _row(norm_c[l])
        for gi in range(2):
            x, mod = xs[gi], mods[gi]
            q, k, v, z, ga, pb, pc = _inproj(x, mod, _row(g_pre_mix[l]), w_in_p, conv_a[l], consts["ones512"],
                                             alog_row, dtb_row)
            oaf, oab = _mixer_a(q, k, v, ga, consts)
            obf, obb, ocf, ocb = _mixer_bc(pb, pc, wup_pad, bgla_rows, big_row, bfg_row, consts)
            x = _outproj(x, mod, oaf, oab, z, obf, obb, pb, ocf, ocb, pc, na_row, nb_row, nc_row, w_out_bf,
                         _row(g_post_mix[l]), consts["ones512"])
            x = _ffn(x, mod, _row(g_pre_ffn[l]), w_up_bf, conv_ffn_pair, b_ffn_pair, w_down_bf,
                     _row(g_post_ffn[l]))
            xs[gi] = x
    return (xs[0], xs[1])
```

```python
import functools
import math

import numpy as np
import jax
import jax.numpy as jnp
from jax import lax
from jax.experimental import pallas as pl
from jax.experimental.pallas import tpu as pltpu

F32 = jnp.float32
BF16 = jnp.bfloat16

D_MODEL = 1024
A_HEADS, A_DK, A_DV = 8, 64, 64
B_HEADS, B_DK, B_DV, B_RANK, B_TAU = 4, 32, 64, 16, 16.0
C_HEADS, C_DK, C_DV = 4, 64, 64
D_FF = 2688
EPS = 1e-6

A_QK = A_HEADS * A_DK
A_VW = A_HEADS * A_DV
A_CONV_CH = 2 * A_QK + A_VW
B_QK = B_HEADS * B_DK
B_VW = B_HEADS * B_DV
C_QK = C_HEADS * C_DK
C_VW = C_HEADS * C_DV

LANES = 128
SUBLANES = 8
CHUNK = 64
GROUP = 4
GW = GROUP * 64

W_REST_A = A_VW + LANES
W_B = 2 * B_QK + 2 * B_VW + LANES
W_C = 2 * C_QK + 2 * C_VW + LANES
W_IN_PAD = A_CONV_CH + W_REST_A + W_B + W_C

T_PROJ = 512
T_MIX = 1024
T_OUT = 1024
T_FFN = 1024
F_TILE = 896
FFN_LAG = 4
VMEM_LIMIT = 56 * 1024 * 1024
MIX_OUT_DTYPE = BF16
A_CHUNKS_PER_ITER = 8
A_WAVE = 4
BC_CHUNKS_PER_ITER = 4


def _dot(a, b):
    return jnp.dot(a, b, preferred_element_type=F32)


def _dot_nt(a, b):
    return lax.dot_general(a, b, (((1,), (1,)), ((), ())), preferred_element_type=F32)


def _dot_tn(a, b):
    return lax.dot_general(a, b, (((0,), (0,)), ((), ())), preferred_element_type=F32)


def _bf(x):
    return x.astype(BF16)


def _split2(x):
    hi = _bf(x)
    lo = _bf(x - hi.astype(F32))
    return hi, lo


def _cdot_hl(c, x):
    hi, lo = _split2(x)
    return _dot(c, hi) + _dot(c, lo)


def _dotc_hl(x, c):
    hi, lo = _split2(x)
    return _dot(hi, c) + _dot(lo, c)


def _tile_rows(x, n):
    return jnp.concatenate([x] * n, axis=0)


def _silu(x):
    return x * (1.0 / (1.0 + jnp.exp(-x)))


def _sigmoid(x):
    return 1.0 / (1.0 + jnp.exp(-x))


def _softplus(x):
    return jnp.maximum(x, 0.0) + jnp.log(1.0 + jnp.exp(-jnp.abs(x)))


def _log_sigmoid(x):
    return -_softplus(-x)


def _rms(x, g_row):
    return x * lax.rsqrt(jnp.mean(x * x, axis=-1, keepdims=True) + EPS) * g_row


def _params(sem):
    return pltpu.CompilerParams(dimension_semantics=sem, vmem_limit_bytes=VMEM_LIMIT)


def _dir_masks(rev):
    i = np.arange(CHUNK)[:, None]
    j = np.arange(CHUNK)[None, :]
    if rev:
        i, j = CHUNK - 1 - i, CHUNK - 1 - j
    incl = j <= i
    strict = j < i
    rows = [incl, strict, np.eye(CHUNK, dtype=bool), (i // 8 == j // 8) & strict]
    for s in (8, 16, 32, 1, 2, 4):
        rows.append((i // (2 * s) == j // (2 * s)) & (i // s == j // s + 1))
    return np.stack([np.tile(r, (1, GROUP)) for r in rows]).astype(np.float32), incl.astype(np.float32)


M_INCL, M_STRICT, M_EYE, M_BLK8, M_OFF8, M_OFF16, M_OFF32, M_OFF1, M_OFF2, M_OFF4 = range(10)
_OFF_INDEX = {8: M_OFF8, 16: M_OFF16, 32: M_OFF32, 1: M_OFF1, 2: M_OFF2, 4: M_OFF4}


def _gla_cumsum_mats(rev):
    i = np.arange(CHUNK)[:, None]
    t = np.arange(CHUNK)[None, :]
    mats = []
    for s in (32, 16, 8, 4, 2, 1):
        blk = (i // (2 * s)) * 2 * s
        upper = (i % (2 * s)) >= s
        csq = upper & (t >= blk + s) & (t <= i)
        csk = (~upper) & (t >= i + 1) & (t <= blk + s - 1)
        if rev:
            csq, csk = csq[::-1, ::-1], csk[::-1, ::-1]
        mats += [csq, csk]
    return mats


GLA_LEVELS = (32, 16, 8, 4, 2, 1)


def _constants():
    c = {}
    for d, rev in enumerate((False, True)):
        masks, tri = _dir_masks(rev)
        c[f"masks{d}"] = jnp.asarray(masks)
        c[f"tri{d}"] = jnp.asarray(tri, BF16)
        gl = np.concatenate([tri] + [m.astype(np.float32) for m in _gla_cumsum_mats(rev)], axis=0)
        c[f"glacs{d}"] = jnp.asarray(gl, BF16)
    r = np.arange(GW)
    c["bm"] = jnp.asarray((r[:, None] // 64 == r[None, :] // 64), BF16)
    c["bm_f32"] = jnp.asarray((r[:, None] // 64 == r[None, :] // 64), F32)
    c["bmk"] = jnp.asarray((r[:, None] // 64 == np.arange(B_QK)[None, :] // B_DK), BF16)
    c["bmk_f32"] = jnp.asarray((r[:, None] // 64 == np.arange(B_QK)[None, :] // B_DK), F32)
    r5 = np.arange(512)
    c["ones512"] = jnp.asarray((r5[:, None] // 64 == r5[None, :] // 64), BF16)
    for d in range(2):
        ex = np.zeros((LANES, 2 * A_VW), np.float32)
        for q in range(2):
            for h in range(A_HEADS):
                ex[16 * q + 8 * d + h, q * A_VW + h * 64: q * A_VW + (h + 1) * 64] = 1.0
        c[f"exp_a{d}"] = jnp.asarray(ex, BF16)
    for d in range(2):
        ex = np.zeros((LANES, 2 * C_VW), np.float32)
        for q in range(2):
            for h in range(C_HEADS):
                ex[8 * q + 4 * d + h, q * C_VW + h * 64: q * C_VW + (h + 1) * 64] = 1.0
        c[f"exp_c{d}"] = jnp.asarray(ex, BF16)
    return c


def _mod_kernel(c_ref, w_ref, b_ref, o_ref):
    c = c_ref[...]
    s = _silu(c)
    s_hi, s_lo = _split2(s)
    w = w_ref[...]
    w_hi, w_lo = _split2(w)
    o_ref[...] = _dot(s_hi, w_hi) + _dot(s_hi, w_lo) + _dot(s_lo, w_hi) + b_ref[...]


def _mod(c_all, w_ada_l, b_ada_l):
    n, d = c_all.shape
    n6 = w_ada_l.shape[1]
    tn = 1536
    return pl.pallas_call(
        _mod_kernel,
        grid=(n6 // tn,),
        in_specs=[pl.BlockSpec((n, d), lambda j: (0, 0)),
                  pl.BlockSpec((d, tn), lambda j: (0, j)),
                  pl.BlockSpec((1, tn), lambda j: (0, j))],
        out_specs=pl.BlockSpec((n, tn), lambda j: (0, j)),
        out_shape=jax.ShapeDtypeStruct((n, n6), F32),
        compiler_params=_params(("arbitrary",)),
        name="adaln_mod",
    )(c_all, w_ada_l, b_ada_l.reshape(1, n6))


def _seg_sum64(x, ones_bd):
    return _dot(_bf(x), ones_bd)


def _inproj_kernel(xp_ref, x_ref, xn_ref, mod_ref, g_ref, w_ref, conv_ref, ones_ref, alog_ref, dtb_ref,
                   q_ref, k_ref, v_ref, z_ref, ga_ref, pb_ref, pc_ref, scr_ref):
    i = pl.program_id(1)
    nt = pl.num_programs(1)
    t = x_ref.shape[1]
    mod = mod_ref[0]
    sh1, sc1 = mod[0:1], mod[1:2]
    xe = jnp.concatenate([xp_ref[0], x_ref[0], xn_ref[0]], axis=0)
    h = _bf(_rms(xe, g_ref[...]) * (1.0 + sc1) + sh1)
    lo_ok = (i > 0).astype(F32)
    hi_ok = (i < nt - 1).astype(F32)
    for part in range(A_CONV_CH // A_QK):
        cols = slice(part * A_QK, (part + 1) * A_QK)
        p = _dot(h, w_ref[:, cols])
        scr_ref[0:SUBLANES, cols] = p[0:SUBLANES] * lo_ok
        scr_ref[SUBLANES:t + SUBLANES, cols] = p[SUBLANES:t + SUBLANES]
        scr_ref[t + SUBLANES:t + 2 * SUBLANES, cols] = p[t + SUBLANES:] * hi_ok
    hm = h[SUBLANES:t + SUBLANES]
    rest = _dot(hm, w_ref[:, A_CONV_CH:])
    ones_bd = ones_ref[...]

    def conv_silu(cols):
        cw = conv_ref[:, cols]
        return _silu(cw[0:1] * scr_ref[pl.ds(SUBLANES - 1, t), cols] + cw[1:2] * scr_ref[pl.ds(SUBLANES, t), cols]
                     + cw[2:3] * scr_ref[pl.ds(SUBLANES + 1, t), cols])

    q = conv_silu(slice(0, A_QK))
    q_ref[0] = q * lax.rsqrt(_seg_sum64(q * q, ones_bd) + EPS) * (A_DK ** -0.5)
    k = conv_silu(slice(A_QK, 2 * A_QK))
    k_ref[0] = k * lax.rsqrt(_seg_sum64(k * k, ones_bd) + EPS)
    v_ref[0] = conv_silu(slice(2 * A_QK, A_CONV_CH))
    z_ref[0] = rest[:, 0:A_VW].astype(z_ref.dtype)
    gt = rest[:, A_VW:W_REST_A]
    col = lax.broadcasted_iota(jnp.int32, gt.shape, 1)
    gdec = -jnp.exp(alog_ref[...]) * _softplus(gt + dtb_ref[...])
    ga_ref[0] = jnp.where(col < 2 * A_HEADS, gdec, _sigmoid(gt))
    pb_ref[0] = rest[:, W_REST_A:W_REST_A + W_B]
    pc_ref[0] = rest[:, W_REST_A + W_B:]


def _inproj(x, mod, g_pre, w_in_p, conv_a, ones512, alog_row, dtb_row):
    b, s, d = x.shape
    t = T_PROJ
    nt = s // t
    tb = t // SUBLANES
    nblk8 = s // SUBLANES
    shapes = [(A_QK, F32), (A_QK, F32), (A_VW, F32), (A_VW, MIX_OUT_DTYPE), (LANES, F32), (W_B, F32), (W_C, F32)]
    full = lambda shape: pl.BlockSpec(shape, lambda bi, i: (0,) * len(shape))
    return pl.pallas_call(
        _inproj_kernel,
        grid=(b, nt),
        in_specs=[pl.BlockSpec((1, SUBLANES, d), lambda bi, i: (bi, jnp.maximum(i * tb - 1, 0), 0)),
                  pl.BlockSpec((1, t, d), lambda bi, i: (bi, i, 0)),
                  pl.BlockSpec((1, SUBLANES, d), lambda bi, i: (bi, jnp.minimum((i + 1) * tb, nblk8 - 1), 0)),
                  pl.BlockSpec((1, 6, d), lambda bi, i: (bi, 0, 0)),
                  full((1, d)), full((d, W_IN_PAD)), full((3, A_CONV_CH)), full((512, 512)),
                  full((1, LANES)), full((1, LANES))],
        out_specs=[pl.BlockSpec((1, t, w), lambda bi, i: (bi, i, 0)) for w, _ in shapes],
        out_shape=[jax.ShapeDtypeStruct((b, s, w), dt) for w, dt in shapes],
        scratch_shapes=[pltpu.VMEM((t + 2 * SUBLANES, A_CONV_CH), F32)],
        compiler_params=_params(("parallel", "arbitrary")),
        name="inproj",
    )(x, x, x, mod, g_pre, w_in_p, conv_a, ones512, alog_row, dtb_row)


def _bd(x_bf, bm_bf):
    zero = jnp.zeros((CHUNK, LANES), BF16)
    blocks = []
    for h in range(GROUP):
        t = (h * 64) // LANES
        lanes = slice(t * LANES, (t + 1) * LANES)
        piece = x_bf[:, lanes] * bm_bf[h * CHUNK:(h + 1) * CHUNK, lanes]
        blocks.append(jnp.concatenate([piece, zero] if t == 0 else [zero, piece], axis=1))
    return jnp.concatenate(blocks, axis=0)


def _hdot(a, b_bf_bd):
    return _dot(_bf(a), b_bf_bd)


_PAIRS = tuple(slice(p * LANES, (p + 1) * LANES) for p in range(GW // LANES))


def _chunk_rows(c, n_chunks, rev):
    idx = (n_chunks - 1 - c) if rev else c
    return pl.multiple_of(idx * CHUNK, CHUNK)


def _where_mask(m, x, other):
    return jnp.where(m > 0.5, x, other)


def _otimes(a, b, bm):
    return _dot(_bf(a), _bd(_bf(b), bm))


def _delta_local(q, k, v, g_e, b_e, masks, tri, bm, rev):
    gam = _cdot_hl(tri, g_e)
    kb = k * b_e
    kq = _dot_nt(_bf(jnp.concatenate([kb, q], axis=0)), _bd(_bf(k), bm))
    yield
    gj = jnp.sum(gam * masks[M_EYE], axis=0, keepdims=True)
    decay = jnp.exp(_where_mask(masks[M_INCL], gam - gj, -jnp.inf))
    eg = jnp.exp(gam)
    m = kq[0:CHUNK] * decay * masks[M_STRICT]
    aqk = kq[CHUNK:] * decay
    last = 0 if rev else CHUNK - 1
    glast = gam[last:last + 1]
    qd = q * eg
    kd = k * jnp.exp(glast - gam)
    cd = jnp.exp(glast)
    d = m * masks[M_BLK8]
    p1 = _otimes(d, d, bm)
    yield
    x = masks[M_EYE] - d
    both = _otimes(jnp.concatenate([p1, x], axis=0), p1, bm)
    p2 = both[0:CHUNK]
    x = x + both[CHUNK:]
    yield
    x = x + _otimes(x, p2, bm)
    yield
    for s in (8, 16, 32):
        e = m * masks[_OFF_INDEX[s]]
        ex = _otimes(e, x, bm)
        yield
        x = x - _otimes(x, ex, bm)
        yield
    wv = _otimes(x, v * b_e, bm)
    kc = _otimes(x, kb * eg, bm)
    yield
    return wv, kc, aqk, qd, kd, cd


def _delta_step(local, state_ref, idx, o_ref, rows, cols, bm, bm_f32):
    wv, kc, aqk, qd, kd, cd = local
    lhs = _bf(jnp.concatenate([kc, qd], axis=0))
    pairs = [slice(p * LANES, (p + 1) * LANES) for p in range(GW // LANES)]
    states = [state_ref[idx * len(pairs) + p] for p in range(len(pairs))]
    ks_qs = jnp.concatenate([_dot(lhs[:, lanes], _bf(st)) for lanes, st in zip(pairs, states)], axis=1)
    yield
    vnew_bf = _bf(wv - ks_qs[0:CHUNK])
    o_ref[0, rows, cols] = (ks_qs[CHUNK:] + _dot(_bf(aqk), _bd(vnew_bf, bm))).astype(o_ref.dtype)
    kd_bf = _bf(kd)
    pair_mask = bm_f32[0:LANES, 0:LANES]
    for p, (lanes, st) in enumerate(zip(pairs, states)):
        state_ref[idx * len(pairs) + p] = (st * cd[:, lanes]
                                           + _dot_tn(kd_bf[:, lanes], vnew_bf[:, lanes]) * pair_mask)
    yield


def _interleave(gens):
    out = [None] * len(gens)
    live = list(enumerate(gens))
    while live:
        nxt = []
        for i, g in live:
            try:
                next(g)
                nxt.append((i, g))
            except StopIteration as stop:
                out[i] = stop.value
        live = nxt
    return out


def _mixer_a_kernel(qf_ref, kf_ref, vf_ref, gf_ref, qb_ref, kb_ref, vb_ref, gb_ref,
                    e0_ref, e1_ref, m0_ref, m1_ref, t0_ref, t1_ref, bm_ref, bmf_ref,
                    of_ref, ob_ref, state_ref, ge_ref):
    tb = qf_ref.shape[1]
    n_chunks = tb // CHUNK
    n_groups = A_HEADS // GROUP

    @pl.when(pl.program_id(1) == 0)
    def _():
        state_ref[...] = jnp.zeros_like(state_ref)

    for d, (g_ref, e_ref) in enumerate(((gf_ref, e0_ref), (gb_ref, e1_ref))):
        ge_ref[d, :, 0:A_VW] = _dotc_hl(g_ref[0], e_ref[:, 0:A_VW])
        ge_ref[d, :, A_VW:] = _dot(_bf(g_ref[0]), e_ref[:, A_VW:])
    bm = bm_ref[...]
    bm_f32 = bmf_ref[...]
    dirs = ((qf_ref, kf_ref, vf_ref, of_ref, m0_ref, t0_ref), (qb_ref, kb_ref, vb_ref, ob_ref, m1_ref, t1_ref))

    def local_gen(unit):
        j, d, hg, rows, cols = unit
        return _delta_local(dirs[d][0][0, rows, cols], dirs[d][1][0, rows, cols], dirs[d][2][0, rows, cols],
                            ge_ref[d, rows, cols], ge_ref[d, rows, A_VW + hg * GW:A_VW + (hg + 1) * GW],
                            dirs[d][4], dirs[d][5][...], bm, d == 1)

    def steps(units, local):
        for j in sorted({u[0] for u in units}):
            live = [_delta_step(local[i], state_ref, d * n_groups + hg, dirs[d][3], rows, cols, bm, bm_f32)
                    for i, (ju, d, hg, rows, cols) in enumerate(units) if ju == j]
            while live:
                nxt = []
                for g in live:
                    try:
                        next(g)
                        nxt.append(g)
                    except StopIteration:
                        pass
                live = nxt
                yield

    def body(it, carry):
        prev = None
        for w in range(A_CHUNKS_PER_ITER // A_WAVE):
            units = []
            for j in range(w * A_WAVE, (w + 1) * A_WAVE):
                for d in range(2):
                    rows = pl.ds(_chunk_rows(it * A_CHUNKS_PER_ITER + j, n_chunks, d == 1), CHUNK)
                    for hg in range(n_groups):
                        units.append((j, d, hg, rows, slice(hg * GW, (hg + 1) * GW)))
            gens = [local_gen(u) for u in units]
            if prev is not None:
                gens.append(steps(*prev))
            prev = (units, _interleave(gens)[:len(units)])
        _interleave([steps(*prev)])
        return carry

    lax.fori_loop(0, n_chunks // A_CHUNKS_PER_ITER, body, 0)


def _mixer_a(q, k, v, ga, consts):
    b, s, _ = q.shape
    tb = T_MIX
    nb = s // tb
    fwd = lambda bi, i: (bi, i, 0)
    bwd = lambda bi, i: (bi, nb - 1 - i, 0)
    full = lambda arr: pl.BlockSpec(arr.shape, lambda bi, i: (0,) * arr.ndim)
    blk = lambda w, m: pl.BlockSpec((1, tb, w), m)
    cs = [consts["exp_a0"], consts["exp_a1"], consts["masks0"], consts["masks1"], consts["tri0"], consts["tri1"],
          consts["bm"], consts["bm_f32"]]
    return pl.pallas_call(
        _mixer_a_kernel,
        grid=(b, nb),
        in_specs=[blk(A_QK, fwd), blk(A_QK, fwd), blk(A_VW, fwd), blk(LANES, fwd),
                  blk(A_QK, bwd), blk(A_QK, bwd), blk(A_VW, bwd), blk(LANES, bwd)] + [full(a) for a in cs],
        out_specs=[blk(A_VW, fwd), blk(A_VW, bwd)],
        out_shape=[jax.ShapeDtypeStruct((b, s, A_VW), MIX_OUT_DTYPE)] * 2,
        scratch_shapes=[pltpu.VMEM((2 * (A_HEADS // GROUP) * (GW // LANES), LANES, LANES), F32),
                        pltpu.VMEM((2, tb, 2 * A_VW), F32)],
        compiler_params=_params(("parallel", "arbitrary")),
        name="mixer_delta",
    )(q, k, v, ga, q, k, v, ga, *cs)


def _gla_local(q, k, v, g, masks, glacs, bm, bmk, bmk_f32, rev):
    cs = _cdot_hl(glacs, g)
    k_bf = _bf(k)
    attn = _dot_nt(_bf(q), _tile_rows(k_bf, GROUP) * bmk) * masks[M_EYE]
    yield
    b = cs[0:CHUNK]
    for li, s in enumerate(GLA_LEVELS):
        qe = cs[(1 + 2 * li) * CHUNK:(2 + 2 * li) * CHUNK]
        ke = cs[(2 + 2 * li) * CHUNK:(3 + 2 * li) * CHUNK]
        kl = _tile_rows(_bf(k * jnp.exp(ke)), GROUP) * bmk
        attn = attn + _dot_nt(_bf(q * jnp.exp(qe)), kl) * masks[_OFF_INDEX[s]]
        if li % 2 == 1:
            yield
    last = 0 if rev else CHUNK - 1
    blast = b[last:last + 1]
    qd_bf = _bf(q * jnp.exp(b))
    kd = k * jnp.exp(blast - b)
    cd = jnp.exp(blast)
    v_bf = _bf(v)
    o_intra = _dot(_bf(attn), _bd(v_bf, bm))
    upd = _dot_tn(v_bf, _bf(kd)) * bmk_f32
    yield
    return o_intra, qd_bf, upd, cd


def _gla_step(local, state_ref, d, o_ref, rows):
    o_intra, qd_bf, upd, cd = local
    state_t = state_ref[d]
    o_ref[0, rows, :] = (o_intra + _dot_nt(qd_bf, _bf(state_t))).astype(o_ref.dtype)
    state_ref[d] = state_t * cd + upd
    yield


def _cummax_rows(u, rev):
    n = u.shape[0]
    row = lax.broadcasted_iota(jnp.int32, u.shape, 0)
    x = u
    step = 1
    while step < n:
        if step % SUBLANES == 0:
            fill = jnp.full((step,) + u.shape[1:], -jnp.inf, u.dtype)
            sh = jnp.concatenate([x[step:], fill] if rev else [fill, x[:n - step]], axis=0)
        elif rev:
            sh = jnp.where(row < n - step, pltpu.roll(x, n - step, 0), -jnp.inf)
        else:
            sh = jnp.where(row >= step, pltpu.roll(x, step, 0), -jnp.inf)
        x = jnp.maximum(x, sh)
        step *= 2
    return x


def _mlstm_local(q, k, v, ip_e, lf_e, masks, tri, bm, bm_f32, rev):
    fc = _cdot_hl(tri, lf_e)
    q_bf = _bf(q)
    qk = _dot_nt(q_bf, _bd(_bf(k), bm))
    yield
    u = ip_e - fc
    cm = _cummax_rows(u, rev)
    dmax = fc + cm
    uj = jnp.sum(u * masks[M_EYE], axis=0, keepdims=True)
    p_bf = _bf(qk * jnp.exp(_where_mask(masks[M_INCL], uj - cm, -jnp.inf)))
    last = 0 if rev else CHUNK - 1
    flast = fc[last:last + 1]
    gmax = flast + cm[last:last + 1]
    v_bf = _bf(v)
    pv = _dot(p_bf, _bd(v_bf, bm))
    psum = _dot(p_bf, bm)
    kg = k * jnp.exp(flast + u - gmax)
    kg_bf = _bf(kg)
    pair_mask = bm_f32[0:LANES, 0:LANES]
    w = [_dot_tn(kg_bf[:, lanes], v_bf[:, lanes]) * pair_mask for lanes in _PAIRS]
    ksum = jnp.sum(kg, axis=0, keepdims=True)
    yield
    return q, q_bf, fc, dmax, flast, gmax, pv, psum, w, ksum


def _mlstm_step(local, c_ref, n_ref, mx_ref, d, o_ref, rows, bm):
    q, q_bf, fc, dmax, flast, gmax, pv, psum, w, ksum = local
    nrow, mrow = n_ref[d], mx_ref[d]
    cmats = [c_ref[d * len(_PAIRS) + p] for p in range(len(_PAIRS))]
    qc = jnp.concatenate([_dot(q_bf[:, lanes], _bf(cm)) for lanes, cm in zip(_PAIRS, cmats)], axis=1)
    qn = _dot(_bf(q * nrow), bm)
    yield
    m_inter = fc + mrow
    m_i = jnp.maximum(dmax, m_inter)
    s_inter = jnp.exp(m_inter - m_i)
    s_intra = jnp.exp(dmax - m_i)
    num = s_inter * qc + s_intra * pv
    den = s_inter * qn + s_intra * psum
    o_ref[0, rows, :] = (num / jnp.maximum(jnp.abs(den), jnp.exp(-m_i))).astype(o_ref.dtype)
    m_new = jnp.maximum(flast + mrow, gmax)
    a = jnp.exp(flast + mrow - m_new)
    sc = jnp.exp(gmax - m_new)
    for p, (lanes, cm) in enumerate(zip(_PAIRS, cmats)):
        c_ref[d * len(_PAIRS) + p] = a[:, lanes] * cm + sc[:, lanes] * w[p]
    n_ref[d] = a * nrow + sc * ksum
    mx_ref[d] = m_new
    yield


def _mixer_bc_kernel(bf_ref, bb_ref, cf_ref, cb_ref, wup_ref, bg_ref, e0_ref, e1_ref, big_ref, bfg_ref,
                     m0_ref, m1_ref, t0_ref, t1_ref, g0_ref, g1_ref, bm_ref, bmf_ref, bmk_ref, bmkf_ref,
                     obf_ref, obb_ref, ocf_ref, ocb_ref, sb_ref, gk_ref, c_ref, n_ref, mx_ref, ge_ref):
    tb = bf_ref.shape[1]
    n_chunks = tb // CHUNK

    @pl.when(pl.program_id(1) == 0)
    def _():
        sb_ref[...] = jnp.zeros_like(sb_ref)
        c_ref[...] = jnp.zeros_like(c_ref)
        n_ref[...] = jnp.zeros_like(n_ref)
        mx_ref[...] = jnp.zeros_like(mx_ref)

    lr_col = 2 * B_QK + 2 * B_VW
    gate_col = 2 * C_QK + 2 * C_VW
    for d, (pb_ref, pc_ref, e_ref) in enumerate(((bf_ref, cf_ref, e0_ref), (bb_ref, cb_ref, e1_ref))):
        l_hi, l_lo = _split2(pb_ref[0, :, lr_col:lr_col + LANES])
        w_hi, w_lo = _split2(wup_ref[d])
        zz = _dot(l_hi, w_hi) + _dot(l_hi, w_lo) + _dot(l_lo, w_hi) + bg_ref[d]
        gk_ref[d] = _log_sigmoid(zz) * (1.0 / B_TAU)
        gt = pc_ref[0, :, gate_col:gate_col + LANES]
        col = lax.broadcasted_iota(jnp.int32, gt.shape, 1)
        gates = jnp.where(col < 2 * C_HEADS, gt + big_ref[...], _log_sigmoid(gt + bfg_ref[...]))
        ge_ref[d] = _dotc_hl(gates, e_ref[...])
    bm = bm_ref[...]
    bm_f32 = bmf_ref[...]
    bmk = bmk_ref[...]
    bmk_f32 = bmkf_ref[...]
    dirs = ((bf_ref, cf_ref, obf_ref, ocf_ref, m0_ref, t0_ref, g0_ref),
            (bb_ref, cb_ref, obb_ref, ocb_ref, m1_ref, t1_ref, g1_ref))

    def body(it, carry):
        units = []
        for j in range(BC_CHUNKS_PER_ITER):
            for d in range(2):
                units.append((j, d, pl.ds(_chunk_rows(it * BC_CHUNKS_PER_ITER + j, n_chunks, d == 1), CHUNK)))
        gens = []
        for (j, d, rows) in units:
            pb_ref, pc_ref, _, _, m_ref, t_ref, g_ref = dirs[d]
            gens.append(_gla_local(pb_ref[0, rows, 0:B_QK] * (B_DK ** -0.5), pb_ref[0, rows, B_QK:2 * B_QK],
                                   pb_ref[0, rows, 2 * B_QK:2 * B_QK + B_VW], gk_ref[d, rows, :],
                                   m_ref, g_ref[...], bm, bmk, bmk_f32, d == 1))
            gens.append(_mlstm_local(pc_ref[0, rows, 0:C_QK], pc_ref[0, rows, C_QK:2 * C_QK] * (C_DK ** -0.5),
                                     pc_ref[0, rows, 2 * C_QK:2 * C_QK + C_VW],
                                     ge_ref[d, rows, 0:C_VW], ge_ref[d, rows, C_VW:2 * C_VW],
                                     m_ref, t_ref[...], bm, bm_f32, d == 1))
        local = _interleave(gens)
        for j in range(BC_CHUNKS_PER_ITER):
            steps = []
            for u, (ju, d, rows) in enumerate(units):
                if ju == j:
                    steps.append(_gla_step(local[2 * u], sb_ref, d, dirs[d][2], rows))
                    steps.append(_mlstm_step(local[2 * u + 1], c_ref, n_ref, mx_ref, d, dirs[d][3], rows, bm))
            _interleave(steps)
        return carry

    lax.fori_loop(0, n_chunks // BC_CHUNKS_PER_ITER, body, 0)


def _mixer_bc(pb, pc, wup_pad, bgla_rows, big_row, bfg_row, consts):
    b, s, _ = pb.shape
    tb = T_MIX
    nb = s // tb
    fwd = lambda bi, i: (bi, i, 0)
    bwd = lambda bi, i: (bi, nb - 1 - i, 0)
    full = lambda arr: pl.BlockSpec(arr.shape, lambda bi, i: (0,) * arr.ndim)
    blk = lambda w, m: pl.BlockSpec((1, tb, w), m)
    cs = [wup_pad, bgla_rows, consts["exp_c0"], consts["exp_c1"], big_row, bfg_row,
          consts["masks0"], consts["masks1"], consts["tri0"], consts["tri1"], consts["glacs0"], consts["glacs1"],
          consts["bm"], consts["bm_f32"], consts["bmk"], consts["bmk_f32"]]
    return pl.pallas_call(
        _mixer_bc_kernel,
        grid=(b, nb),
        in_specs=[blk(W_B, fwd), blk(W_B, bwd), blk(W_C, fwd), blk(W_C, bwd)] + [full(a) for a in cs],
        out_specs=[blk(B_VW, fwd), blk(B_VW, bwd), blk(C_VW, fwd), blk(C_VW, bwd)],
        out_shape=[jax.ShapeDtypeStruct((b, s, B_VW), MIX_OUT_DTYPE)] * 2
        + [jax.ShapeDtypeStruct((b, s, C_VW), MIX_OUT_DTYPE)] * 2,
        scratch_shapes=[pltpu.VMEM((2, B_VW, B_QK), F32), pltpu.VMEM((2, tb, B_QK), F32),
                        pltpu.VMEM((2 * len(_PAIRS), LANES, LANES), F32), pltpu.VMEM((2, 1, GW), F32),
                        pltpu.VMEM((2, 1, GW), F32), pltpu.VMEM((2, tb, 2 * C_VW), F32)],
        compiler_params=_params(("parallel", "arbitrary")),
        name="mixer_gla_mlstm",
    )(pb, pb, pc, pc, *cs)


def _outproj_kernel(x_ref, mod_ref, af_ref, ab_ref, z_ref, bf_ref, bb_ref, bg_ref, cf_ref, cb_ref, co_ref,
                    na_ref, nb_ref, nc_ref, w_ref, gpost_ref, ones_ref, o_ref):
    ones_bd = ones_ref[...]
    oa = af_ref[0].astype(F32) + ab_ref[0].astype(F32)
    ms = _seg_sum64(oa * oa, ones_bd) * (1.0 / A_DV)
    oa = oa * lax.rsqrt(ms + EPS) * na_ref[...] * _silu(z_ref[0].astype(F32))
    ones_q = ones_bd[0:B_VW, 0:B_VW]
    ob = bf_ref[0].astype(F32) + bb_ref[0].astype(F32)
    ms = _seg_sum64(ob * ob, ones_q) * (1.0 / B_DV)
    ob = ob * lax.rsqrt(ms + EPS) * nb_ref[...] * _silu(bg_ref[0])
    hc = _sigmoid(co_ref[0]) * (cf_ref[0].astype(F32) + cb_ref[0].astype(F32))
    mu = _seg_sum64(hc, ones_q) * (1.0 / C_DV)
    dv = hc - mu
    var = _seg_sum64(dv * dv, ones_q) * (1.0 / C_DV)
    oc = dv * lax.rsqrt(var + EPS) * nc_ref[...]
    y = (_dot(_bf(oa), w_ref[0:A_VW, :]) + _dot(_bf(ob), w_ref[A_VW:A_VW + B_VW, :])
         + _dot(_bf(oc), w_ref[A_VW + B_VW:, :]))
    gt1 = mod_ref[0][2:3]
    o_ref[0] = x_ref[0] + gt1 * _rms(y, gpost_ref[...])


def _outproj(x, mod, oaf, oab, z, obf, obb, pb, ocf, ocb, pc, na_row, nb_row, nc_row, w_out_bf, g_post, ones512):
    b, s, d = x.shape
    t = T_OUT
    tok = lambda w, j=0: pl.BlockSpec((1, t, w), lambda bi, i: (bi, i, j))
    full = lambda arr: pl.BlockSpec(arr.shape, lambda bi, i: (0,) * arr.ndim)
    bg_blk = (2 * B_QK + B_VW) // B_VW
    co_blk = (2 * C_QK + C_VW) // C_VW
    return pl.pallas_call(
        _outproj_kernel,
        grid=(b, s // t),
        in_specs=[tok(d), pl.BlockSpec((1, 6, d), lambda bi, i: (bi, 0, 0)),
                  tok(A_VW), tok(A_VW), tok(A_VW), tok(B_VW), tok(B_VW), tok(B_VW, bg_blk),
                  tok(C_VW), tok(C_VW), tok(C_VW, co_blk),
                  full(na_row), full(nb_row), full(nc_row), full(w_out_bf), full(g_post), full(ones512)],
        out_specs=tok(d),
        out_shape=jax.ShapeDtypeStruct((b, s, d), F32),
        compiler_params=_params(("parallel", "arbitrary")),
        name="outproj",
    )(x, mod, oaf, oab, z, obf, obb, pb, ocf, ocb, pc, na_row, nb_row, nc_row, w_out_bf, g_post, ones512)


def _ffn_kernel(xp_ref, x_ref, xn_ref, mod_ref, gpre_ref, wu_ref, cw_ref, cb_ref, wd_ref, gpost_ref,
                o_ref, h_ref, u_ref, acc_ref):
    i = pl.program_id(1)
    nt = pl.num_programs(1)
    f = pl.program_id(2)
    nf = pl.num_programs(2)
    t = x_ref.shape[1]
    n_slabs = wu_ref.shape[1] // (2 * LANES)
    mod = mod_ref[0]

    @pl.when(f == 0)
    def _():
        xe = jnp.concatenate([xp_ref[0], x_ref[0], xn_ref[0]], axis=0)
        h_ref[...] = _bf(_rms(xe, gpre_ref[...]) * (1.0 + mod[4:5]) + mod[3:4])
        acc_ref[...] = jnp.zeros_like(acc_ref)

    h = h_ref[...]
    lo_ok = (i > 0).astype(F32)
    hi_ok = (i < nt - 1).astype(F32)

    def up(j):
        cols = slice(j * 2 * LANES, (j + 1) * 2 * LANES)
        u = _dot(h, wu_ref[:, cols])
        u_ref[0:SUBLANES, cols] = u[0:SUBLANES] * lo_ok
        u_ref[SUBLANES:t + SUBLANES, cols] = u[SUBLANES:t + SUBLANES]
        u_ref[t + SUBLANES:t + 2 * SUBLANES, cols] = u[t + SUBLANES:] * hi_ok

    def gated(j):
        cols = slice(j * 2 * LANES, (j + 1) * 2 * LANES)
        cw = cw_ref[:, cols]
        cv = (cw[0:1] * u_ref[pl.ds(SUBLANES - 1, t), cols] + cw[1:2] * u_ref[pl.ds(SUBLANES, t), cols]
              + cw[2:3] * u_ref[pl.ds(SUBLANES + 1, t), cols] + cb_ref[:, cols])
        gate = cv[:, 0:LANES]
        return _bf(0.5 * gate * (1.0 + lax.erf(gate * (2.0 ** -0.5))) * cv[:, LANES:])

    def down(j0, j1):
        act = jnp.concatenate([gated(j) for j in range(j0, j1)], axis=1)
        acc_ref[...] += _dot(act, wd_ref[j0 * LANES:j1 * LANES, :])

    for j in range(min(FFN_LAG, n_slabs)):
        up(j)
    for j0 in range(0, n_slabs, 2):
        for j in range(j0 + FFN_LAG, min(j0 + FFN_LAG + 2, n_slabs)):
            up(j)
        down(j0, min(j0 + 2, n_slabs))

    @pl.when(f == nf - 1)
    def _():
        o_ref[0] = x_ref[0] + mod[5:6] * _rms(acc_ref[...], gpost_ref[...])


def _pair_gate_value(a):
    lead = a.shape[:-1]
    g = a[..., :D_FF].reshape(lead + (D_FF // LANES, 1, LANES))
    v = a[..., D_FF:].reshape(lead + (D_FF // LANES, 1, LANES))
    return jnp.concatenate([g, v], axis=-2).reshape(lead + (2 * D_FF,))


def _ffn(x, mod, g_pre, w_up_pair_bf, conv_w_pair, conv_b_pair, w_down_bf, g_post):
    b, s, d = x.shape
    t = T_FFN
    ft = F_TILE
    nf = D_FF // ft
    tb = t // SUBLANES
    nblk8 = s // SUBLANES
    full = lambda shape: pl.BlockSpec(shape, lambda bi, i, f: (0,) * len(shape))
    return pl.pallas_call(
        _ffn_kernel,
        grid=(b, s // t, nf),
        in_specs=[pl.BlockSpec((1, SUBLANES, d), lambda bi, i, f: (bi, jnp.maximum(i * tb - 1, 0), 0)),
                  pl.BlockSpec((1, t, d), lambda bi, i, f: (bi, i, 0)),
                  pl.BlockSpec((1, SUBLANES, d), lambda bi, i, f: (bi, jnp.minimum((i + 1) * tb, nblk8 - 1), 0)),
                  pl.BlockSpec((1, 6, d), lambda bi, i, f: (bi, 0, 0)),
                  full((1, d)),
                  pl.BlockSpec((d, 2 * ft), lambda bi, i, f: (0, f)),
                  pl.BlockSpec((3, 2 * ft), lambda bi, i, f: (0, f)),
                  pl.BlockSpec((1, 2 * ft), lambda bi, i, f: (0, f)),
                  pl.BlockSpec((ft, d), lambda bi, i, f: (f, 0)),
                  full((1, d))],
        out_specs=pl.BlockSpec((1, t, d), lambda bi, i, f: (bi, i, 0)),
        out_shape=jax.ShapeDtypeStruct((b, s, d), F32),
        scratch_shapes=[pltpu.VMEM((t + 2 * SUBLANES, d), BF16),
                        pltpu.VMEM((t + 2 * SUBLANES, 2 * ft), F32),
                        pltpu.VMEM((t, d), F32)],
        compiler_params=_params(("parallel", "arbitrary", "arbitrary")),
        name="ffn",
    )(x, x, x, mod, g_pre, w_up_pair_bf, conv_w_pair, conv_b_pair, w_down_bf, g_post)


def _pad_cols(w, sections):
    parts, start = [], 0
    for width, padded in sections:
        parts.append(w[:, start:start + width])
        if padded > width:
            parts.append(jnp.zeros((w.shape[0], padded - width), w.dtype))
        start += width
    return jnp.concatenate(parts, axis=1)


def _row(v, width=None):
    v = v.reshape(1, -1).astype(F32)
    if width is not None and v.shape[1] < width:
        v = jnp.pad(v, ((0, 0), (0, width - v.shape[1])))
    return v


def kernel(x_prompt, x_sample, c_prompt, c_sample, w_ada, b_ada, g_pre_mix, w_in, conv_a, a_log, dt_bias, norm_a,
           w_gla_up, b_gla, norm_b, bias_ig, bias_fg, norm_c, w_out, g_post_mix, g_pre_ffn, w_up, conv_ffn,
           b_conv_ffn, w_down, g_post_ffn):
    consts = _constants()
    depth = w_ada.shape[0]
    d = D_MODEL
    n_prompt = x_prompt.shape[0]
    c_all = jnp.concatenate([c_prompt, c_sample], axis=0)
    xs = [x_prompt, x_sample]
    a_in = A_CONV_CH + A_VW + 4 * A_HEADS
    b_in = 2 * B_QK + 2 * B_VW + 2 * B_RANK
    c_in = 2 * C_QK + 2 * C_VW + 4 * C_HEADS
    for l in range(depth):
        mod_all = _mod(c_all, w_ada[l], b_ada[l])
        mods = [mod_all[:n_prompt].reshape(-1, 6, d), mod_all[n_prompt:].reshape(-1, 6, d)]
        w_in_p = _pad_cols(w_in[l].astype(BF16), [(a_in, A_CONV_CH + W_REST_A), (b_in, W_B), (c_in, W_C)])
        w_out_bf = w_out[l].astype(BF16)
        w_up_bf = _pair_gate_value(w_up[l].astype(BF16))
        conv_ffn_pair = _pair_gate_value(conv_ffn[l])
        b_ffn_pair = _pair_gate_value(_row(b_conv_ffn[l]))
        w_down_bf = w_down[l].astype(BF16)
        alog_row = _row(a_log[l], LANES)
        dtb_row = _row(dt_bias[l], LANES)
        wup_pad = jnp.zeros((2, LANES, B_QK), F32)
        for dd in range(2):
            wup_pad = wup_pad.at[dd, dd * B_RANK:(dd + 1) * B_RANK, :].set(w_gla_up[l, dd])
        bgla_rows = b_gla[l].reshape(2, 1, B_QK).astype(F32)
        big_row = _row(bias_ig[l], LANES)
        bfg_row = jnp.pad(bias_fg[l].reshape(1, -1).astype(F32), ((0, 0), (2 * C_HEADS, LANES - 4 * C_HEADS)))
        na_row = _row(jnp.tile(norm_a[l], A_HEADS))
        nb_row = _row(jnp.tile(norm_b[l], B_HEADS))
        nc_row = _row(norm_c[l])
        for gi in range(2):
            x, mod = xs[gi], mods[gi]
            q, k, v, z, ga, pb, pc = _inproj(x, mod, _row(g_pre_mix[l]), w_in_p, conv_a[l], consts["ones512"],
                                             alog_row, dtb_row)
            oaf, oab = _mixer_a(q, k, v, ga, consts)
            obf, obb, ocf, ocb = _mixer_bc(pb, pc, wup_pad, bgla_rows, big_row, bfg_row, consts)
            x = _outproj(x, mod, oaf, oab, z, obf, obb, pb, ocf, ocb, pc, na_row, nb_row, nc_row, w_out_bf,
                         _row(g_post_mix[l]), consts["ones512"])
            x = _ffn(x, mod, _row(g_pre_ffn[l]), w_up_bf, conv_ffn_pair, b_ffn_pair, w_down_bf,
                     _row(g_post_ffn[l]))
            xs[gi] = x
    return (xs[0], xs[1])
```
